```python
import math
import jax
import jax.numpy as jnp
from jax import lax
import numpy as np

D_MODEL = 2048
BATCH = 4
SEQ = 4096
DEPTH = 4

N_MIXERS = 2
N_A_LAYERS = (DEPTH + 1) // 2
N_B_LAYERS = DEPTH // 2
MLA_HEADS = 16
Q_LORA = 512
KV_LORA = 256
QK_NOPE = 128
QK_ROPE = 64
V_HEAD = 128
ROPE_THETA = 10000.0
Q_BLOCK = 128
MLA_IN_DIM = Q_LORA + KV_LORA + QK_ROPE
POOL_WINDOWS = (2, 4, 8, 16)
POOL_GROUPS = len(POOL_WINDOWS)
POOL_GROUP_DIM = D_MODEL // POOL_GROUPS
D_FF = 256 * ((8 * D_MODEL // 3 + 255) // 256)
N_EXPERTS = 8
TOP_K = 2
D_FF_EXPERT = D_FF
LN_EPS = 1e-5
RMS_EPS = 1e-6
DEEPNORM_ALPHA = (2.0 * DEPTH) ** 0.25
DEEPNORM_BETA = (8.0 * DEPTH) ** -0.25
N_MOD = 6

kernel_name = "hybrid_mla_pool_moe_encoder"

F32 = jnp.float32


def _layernorm(x, g, b):
    xf = x.astype(F32)
    mu = jnp.mean(xf, axis=-1, keepdims=True)
    xc = xf - mu
    var = jnp.mean(xc * xc, axis=-1, keepdims=True)
    return (xc * lax.rsqrt(var + LN_EPS) * g.astype(F32) + b.astype(F32)).astype(x.dtype)


def _rmsnorm(x, g):
    xf = x.astype(F32)
    return (xf * lax.rsqrt(jnp.mean(xf * xf, axis=-1, keepdims=True) + RMS_EPS) * g.astype(F32)).astype(x.dtype)


def _rope_tables(positions):
    inv_freq = ROPE_THETA ** (-jnp.arange(0, QK_ROPE, 2, dtype=F32) / QK_ROPE)
    ang = positions.astype(F32)[..., None] * inv_freq
    return jnp.cos(ang), jnp.sin(ang)


def _apply_rope(t, cos, sin):
    tf = t.astype(F32)
    t1, t2 = jnp.split(tf, 2, axis=-1)
    return jnp.concatenate([t1 * cos - t2 * sin, t2 * cos + t1 * sin], axis=-1).astype(t.dtype)


def _mla(h, cos, sin, w_in, q_norm, w_uq, kv_norm, w_ukv, w_o):
    b, s, _ = h.shape
    lat = h @ w_in
    q_lat, kv_lat, k_rope = jnp.split(lat, [Q_LORA, Q_LORA + KV_LORA], axis=-1)
    q = (_rmsnorm(q_lat, q_norm) @ w_uq).reshape(b, s, MLA_HEADS, QK_NOPE + QK_ROPE)
    q_nope = q[..., :QK_NOPE]
    q_rope = _apply_rope(q[..., QK_NOPE:], cos[:, :, None, :], sin[:, :, None, :])
    kv = (_rmsnorm(kv_lat, kv_norm) @ w_ukv).reshape(b, s, MLA_HEADS, QK_NOPE + V_HEAD)
    k_nope, v = kv[..., :QK_NOPE], kv[..., QK_NOPE:]
    k_rope = _apply_rope(k_rope, cos, sin)
    n_blocks = s // Q_BLOCK
    scale = (QK_NOPE + QK_ROPE) ** -0.5

    def to_blocks(t):
        return jnp.moveaxis(t.reshape(b, n_blocks, Q_BLOCK, *t.shape[2:]), 1, 0)

    def attend(qb):
        qn, qr = qb
        sc = (jnp.einsum('bqhd,bkhd->bhqk', qn, k_nope, preferred_element_type=F32)
              + jnp.einsum('bqhr,bkr->bhqk', qr, k_rope, preferred_element_type=F32))
        p = jax.nn.softmax(sc * scale, axis=-1)
        return jnp.einsum('bhqk,bkhd->bqhd', p.astype(v.dtype), v)

    o = lax.map(attend, (to_blocks(q_nope), to_blocks(q_rope)))
    o = jnp.moveaxis(o, 0, 1).reshape(b, s, MLA_HEADS * V_HEAD)
    return o @ w_o


def _pool_mixer(h, w_groups, ch_scale):
    b, s, d = h.shape
    hf = h.astype(F32)
    csum = jnp.concatenate([jnp.zeros((b, 1, d), F32), jnp.cumsum(hf, axis=1)], axis=1)
    t = jnp.arange(s)
    diffs = []
    for g, w in enumerate(POOL_WINDOWS):
        left = w // 2
        right = w - 1 - left
        lo = jnp.maximum(t - left, 0)
        hi = jnp.minimum(t + right, s - 1)
        cols = slice(g * POOL_GROUP_DIM, (g + 1) * POOL_GROUP_DIM)
        cg = csum[:, :, cols]
        win_sum = jnp.take(cg, hi + 1, axis=1) - jnp.take(cg, lo, axis=1)
        cnt = (hi - lo + 1).astype(F32)[None, :, None]
        diffs.append(win_sum / cnt - hf[:, :, cols])
    dg = jnp.stack(diffs, axis=2).astype(h.dtype)
    y = jnp.einsum('bsgc,gcd->bsgd', dg, w_groups).reshape(b, s, d)
    return y * ch_scale


def _swiglu(h, wg, wu, wd):
    return (jax.nn.silu(h @ wg) * (h @ wu)) @ wd


def _moe(h, router, wg, wu, wd):
    logits = jnp.einsum('bsd,de->bse', h, router, preferred_element_type=F32)
    probs = jax.nn.softmax(logits, axis=-1)
    top_p, top_i = lax.top_k(probs, TOP_K)
    top_p = top_p / jnp.sum(top_p, axis=-1, keepdims=True)
    gates = jnp.sum(jax.nn.one_hot(top_i, N_EXPERTS, dtype=F32) * top_p[..., None], axis=-2).astype(h.dtype)
    y = jnp.zeros_like(h)
    for e in range(N_EXPERTS):
        y = y + gates[..., e:e + 1] * _swiglu(h, wg[e], wu[e], wd[e])
    return y


def setup_inputs(seed: int = 0) -> dict:
    key = jax.random.key(seed)
    ks = jax.random.split(key, 24)
    nrm = jax.random.normal
    d, hq = D_MODEL, MLA_HEADS
    inp = {}
    inp['x'] = nrm(ks[0], (BATCH, SEQ, d), F32)
    inp['c'] = nrm(ks[1], (BATCH, d), F32)
    inp['positions'] = jnp.broadcast_to(jnp.arange(SEQ, dtype=jnp.int32), (BATCH, SEQ))
    inp['mod_w'] = nrm(ks[2], (DEPTH, d, N_MOD * d), F32) * (0.1 * d ** -0.5)
    inp['mod_b'] = nrm(ks[3], (DEPTH, N_MOD * d), F32) * 0.01
    inp['ln_g'] = 1.0 + 0.02 * nrm(ks[4], (DEPTH, 2, d), F32)
    inp['ln_b'] = 0.02 * nrm(ks[5], (DEPTH, 2, d), F32)
    inp['mla_w_in'] = nrm(ks[6], (N_A_LAYERS, d, MLA_IN_DIM), F32) * d ** -0.5
    inp['mla_q_norm'] = 1.0 + 0.02 * nrm(ks[7], (N_A_LAYERS, Q_LORA), F32)
    inp['mla_w_uq'] = nrm(ks[8], (N_A_LAYERS, Q_LORA, hq * (QK_NOPE + QK_ROPE)), F32) * Q_LORA ** -0.5
    inp['mla_kv_norm'] = 1.0 + 0.02 * nrm(ks[9], (N_A_LAYERS, KV_LORA), F32)
    inp['mla_w_ukv'] = nrm(ks[10], (N_A_LAYERS, KV_LORA, hq * (QK_NOPE + V_HEAD)), F32) * KV_LORA ** -0.5
    inp['mla_w_o'] = nrm(ks[11], (N_A_LAYERS, hq * V_HEAD, d), F32) * (DEEPNORM_BETA * (hq * V_HEAD) ** -0.5)
    inp['pool_w'] = nrm(ks[12], (N_B_LAYERS, POOL_GROUPS, POOL_GROUP_DIM, POOL_GROUP_DIM), F32) * (DEEPNORM_BETA * POOL_GROUP_DIM ** -0.5)
    inp['pool_scale'] = 1.0 + 0.1 * nrm(ks[13], (N_B_LAYERS, d), F32)
    inp['ffn_w_gate'] = nrm(ks[14], (N_A_LAYERS, d, D_FF), F32) * d ** -0.5
    inp['ffn_w_up'] = nrm(ks[15], (N_A_LAYERS, d, D_FF), F32) * d ** -0.5
    inp['ffn_w_down'] = nrm(ks[16], (N_A_LAYERS, D_FF, d), F32) * (DEEPNORM_BETA * D_FF ** -0.5)
    inp['moe_router'] = nrm(ks[17], (N_B_LAYERS, d, N_EXPERTS), F32) * d ** -0.5
    inp['moe_w_gate'] = nrm(ks[18], (N_B_LAYERS, N_EXPERTS, d, D_FF_EXPERT), F32) * d ** -0.5
    inp['moe_w_up'] = nrm(ks[19], (N_B_LAYERS, N_EXPERTS, d, D_FF_EXPERT), F32) * d ** -0.5
    inp['moe_w_down'] = nrm(ks[20], (N_B_LAYERS, N_EXPERTS, D_FF_EXPERT, d), F32) * (DEEPNORM_BETA * D_FF_EXPERT ** -0.5)
    return inp


def reference(x, c, positions, mod_w, mod_b, ln_g, ln_b,
              mla_w_in, mla_q_norm, mla_w_uq, mla_kv_norm, mla_w_ukv, mla_w_o,
              pool_w, pool_scale,
              ffn_w_gate, ffn_w_up, ffn_w_down,
              moe_router, moe_w_gate, moe_w_up, moe_w_down):
    b, _, d = x.shape
    cos, sin = _rope_tables(positions)
    c_act = jax.nn.silu(c)
    for i in range(DEPTH):
        j = i // N_MIXERS
        mod = (c_act @ mod_w[i] + mod_b[i]).reshape(b, N_MOD, d)
        sh1, sc1, g1, sh2, sc2, g2 = [mod[:, k, None, :] for k in range(N_MOD)]
        h = x * (1.0 + sc1) + sh1
        if i % N_MIXERS == 0:
            y = _mla(h, cos, sin, mla_w_in[j], mla_q_norm[j], mla_w_uq[j],
                     mla_kv_norm[j], mla_w_ukv[j], mla_w_o[j])
        else:
            y = _pool_mixer(h, pool_w[j], pool_scale[j])
        x = _layernorm(DEEPNORM_ALPHA * x + (1.0 + g1) * y, ln_g[i, 0], ln_b[i, 0])
        h = x * (1.0 + sc2) + sh2
        if i % 2 == 0:
            y = _swiglu(h, ffn_w_gate[j], ffn_w_up[j], ffn_w_down[j])
        else:
            y = _moe(h, moe_router[j], moe_w_gate[j], moe_w_up[j], moe_w_down[j])
        x = _layernorm(DEEPNORM_ALPHA * x + (1.0 + g2) * y, ln_g[i, 1], ln_b[i, 1])
    return x
```

```python
import functools

import jax
import jax.numpy as jnp
from jax import lax
from jax.experimental import pallas as pl
from jax.experimental.pallas import tpu as pltpu

F32 = jnp.float32
BF16 = jnp.bfloat16
U32 = jnp.uint32
I32 = jnp.int32

MLA_HEADS = 16
QK_NOPE = 128
QK_ROPE = 64
V_HEAD = 128
Q_LORA = 512
KV_LORA = 256
ROPE_THETA = 10000.0
POOL_WINDOWS = (2, 4, 8, 16)
TOP_K = 2
LN_EPS = 1e-5
RMS_EPS = 1e-6
N_MOD = 6

QK_DIM = QK_NOPE + QK_ROPE
POOL_HALO = 8
LANES = 128
VMEM_LIMIT = 56 * 1024 * 1024


def _tile(n, pref):
    t = min(n, pref)
    assert n % t == 0, (n, pref)
    return t


def _params(sem, vmem=VMEM_LIMIT):
    return pltpu.CompilerParams(dimension_semantics=sem, vmem_limit_bytes=vmem)


def _dot(a, b, **kw):
    return jnp.dot(a, b, preferred_element_type=F32, **kw)


def _silu(v):
    return v * jax.nn.sigmoid(v)


def _residual_layernorm(x, y, gate, ln_g, ln_b, alpha):
    z = alpha * x + (1.0 + gate) * y
    mu = jnp.mean(z, axis=-1, keepdims=True)
    zc = z - mu
    var = jnp.mean(zc * zc, axis=-1, keepdims=True)
    return zc * lax.rsqrt(var + LN_EPS) * ln_g + ln_b


def _rmsnorm(v, g):
    return v * lax.rsqrt(jnp.mean(v * v, axis=-1, keepdims=True) + RMS_EPS) * g


def _mod_kernel(c_ref, w_ref, b_ref, o_ref):
    ca = _silu(c_ref[...])
    o_ref[...] = _dot(ca, w_ref[...], precision=lax.Precision.HIGHEST) + b_ref[...]


def _modulation(c, mod_w, mod_b):
    depth, d, nd = mod_w.shape
    b = c.shape[0]
    rows = 8
    assert b <= rows
    c_pad = jnp.zeros((rows, d), F32).at[:b].set(c)
    tn = _tile(nd, 1024)
    out = pl.pallas_call(
        _mod_kernel,
        grid=(depth, nd // tn),
        in_specs=[
            pl.BlockSpec((rows, d), lambda i, n: (0, 0)),
            pl.BlockSpec((None, d, tn), lambda i, n: (i, 0, n)),
            pl.BlockSpec((None, 1, tn), lambda i, n: (i, 0, n)),
        ],
        out_specs=pl.BlockSpec((None, rows, tn), lambda i, n: (i, 0, n)),
        out_shape=jax.ShapeDtypeStruct((depth, rows, nd), F32),
        compiler_params=_params(("parallel", "parallel")),
        name="modulation",
    )(c_pad, mod_w, mod_b.reshape(depth, 1, nd))
    return out[:, :b].reshape(depth, b, N_MOD, d)


def _rope_kernel(pos_ref, freq_ref, cos_ref, sin_ref):
    ang = pos_ref[...] * freq_ref[...]
    cos_ref[...] = jnp.cos(ang)
    sin_ref[...] = jnp.sin(ang)


def _rope_tables(positions):
    t = positions.size
    half = QK_ROPE // 2
    inv_freq = ROPE_THETA ** (-jnp.arange(0, QK_ROPE, 2, dtype=F32) / QK_ROPE)
    freq = jnp.tile(inv_freq, LANES // half).reshape(1, LANES)
    pos = positions.astype(F32).reshape(t, 1)
    tm = _tile(t, 2048)
    return pl.pallas_call(
        _rope_kernel,
        grid=(t // tm,),
        in_specs=[pl.BlockSpec((tm, 1), lambda i: (i, 0)), pl.BlockSpec((1, LANES), lambda i: (0, 0))],
        out_specs=[pl.BlockSpec((tm, LANES), lambda i: (i, 0))] * 2,
        out_shape=[jax.ShapeDtypeStruct((t, LANES), F32)] * 2,
        compiler_params=_params(("parallel",)),
        name="rope_tables",
    )(pos, freq)


def _mla_proj_kernel(x_ref, mod_ref, cos_ref, sin_ref, win_ref, qn_ref, wqn_ref, wqr_ref, wqt_ref,
                     kvn_ref, wk_ref, wv_ref, q_ref, k_ref, v_ref, *, scale):
    sh, sc = mod_ref[0:1, :], mod_ref[1:2, :]
    h = (x_ref[...] * (1.0 + sc) + sh).astype(BF16)
    lat = _dot(h, win_ref[...])
    cos, sin = cos_ref[...], sin_ref[...]
    qn = _rmsnorm(lat[:, :Q_LORA], qn_ref[...]).astype(BF16)
    kn = _rmsnorm(lat[:, Q_LORA:Q_LORA + KV_LORA], kvn_ref[...]).astype(BF16)
    r0 = Q_LORA + KV_LORA
    k_rope = (lat[:, r0:r0 + QK_ROPE] * cos[:, :QK_ROPE]
              + lat[:, r0 + QK_ROPE:r0 + 2 * QK_ROPE] * sin[:, :QK_ROPE]).astype(BF16)
    q_nope = (_dot(qn, wqn_ref[...]) * scale).astype(BF16)
    q_r, q_t = _dot(qn, wqr_ref[...]), _dot(qn, wqt_ref[...])
    k_nope = _dot(kn, wk_ref[...]).astype(BF16)
    v_ref_val = _dot(kn, wv_ref[...]).astype(BF16)
    heads_per_vreg = LANES // QK_ROPE
    for hp in range(MLA_HEADS // heads_per_vreg):
        cols = slice(hp * LANES, (hp + 1) * LANES)
        roped = ((q_r[:, cols] * cos + q_t[:, cols] * sin) * scale).astype(BF16)
        for sub in range(heads_per_vreg):
            hd = hp * heads_per_vreg + sub
            q_ref[hd, :, :QK_NOPE] = q_nope[:, hd * QK_NOPE:(hd + 1) * QK_NOPE]
            q_ref[hd, :, QK_NOPE:] = roped[:, sub * QK_ROPE:(sub + 1) * QK_ROPE]
    for hd in range(MLA_HEADS):
        k_ref[hd, :, :QK_NOPE] = k_nope[:, hd * QK_NOPE:(hd + 1) * QK_NOPE]
        k_ref[hd, :, QK_NOPE:] = k_rope
        v_ref[hd] = v_ref_val[:, hd * V_HEAD:(hd + 1) * V_HEAD]


def _rotate_half_cols(w):
    lead = w.shape[0]
    w = w.reshape(lead, -1, 2, QK_ROPE // 2)
    return jnp.concatenate([-w[:, :, 1:], w[:, :, :1]], axis=2).reshape(lead, -1)


def _mla_project(x, mod, cos, sin, w_in, q_norm, w_uq, kv_norm, w_ukv, batch, seq):
    t, d = x.shape
    hq = MLA_HEADS
    w_rope = w_in[:, Q_LORA + KV_LORA:]
    win = jnp.concatenate([w_in, _rotate_half_cols(w_rope)], axis=1).astype(BF16)
    wq = w_uq.reshape(Q_LORA, hq, QK_DIM)
    wq_nope = wq[:, :, :QK_NOPE].reshape(Q_LORA, hq * QK_NOPE).astype(BF16)
    wq_rope = wq[:, :, QK_NOPE:].reshape(Q_LORA, hq * QK_ROPE)
    wq_rot = _rotate_half_cols(wq_rope).astype(BF16)
    wq_rope = wq_rope.astype(BF16)
    wkv = w_ukv.reshape(KV_LORA, hq, QK_NOPE + V_HEAD)
    wk = wkv[:, :, :QK_NOPE].reshape(KV_LORA, hq * QK_NOPE).astype(BF16)
    wv = wkv[:, :, QK_NOPE:].reshape(KV_LORA, hq * V_HEAD).astype(BF16)
    tm = _tile(seq, 256)
    per_b = seq // tm
    full = lambda a: pl.BlockSpec(a.shape, lambda i: (0,) * a.ndim)
    qn2, kvn2 = q_norm.reshape(1, -1), kv_norm.reshape(1, -1)
    head_spec = lambda w: pl.BlockSpec((None, hq, tm, w), lambda i: (i // per_b, 0, i % per_b, 0))
    return pl.pallas_call(
        functools.partial(_mla_proj_kernel, scale=float(QK_DIM) ** -0.5),
        grid=(t // tm,),
        in_specs=[
            pl.BlockSpec((tm, d), lambda i: (i, 0)),
            pl.BlockSpec((None, N_MOD, d), lambda i: (i // per_b, 0, 0)),
            pl.BlockSpec((tm, LANES), lambda i: (i, 0)),
            pl.BlockSpec((tm, LANES), lambda i: (i, 0)),
            full(win), full(qn2), full(wq_nope), full(wq_rope), full(wq_rot), full(kvn2), full(wk), full(wv),
        ],
        out_specs=[head_spec(QK_DIM), head_spec(QK_DIM), head_spec(V_HEAD)],
        out_shape=[
            jax.ShapeDtypeStruct((batch, hq, seq, QK_DIM), BF16),
            jax.ShapeDtypeStruct((batch, hq, seq, QK_DIM), BF16),
            jax.ShapeDtypeStruct((batch, hq, seq, V_HEAD), BF16),
        ],
        compiler_params=_params(("parallel",)),
        name="mla_project",
    )(x, mod, cos, sin, win, qn2, wq_nope, wq_rope, wq_rot, kvn2, wk, wv)


def _attn_kernel(q_ref, k_ref, v_ref, o_ref):
    s = lax.dot_general(q_ref[...], k_ref[...], (((1,), (1,)), ((), ())), preferred_element_type=F32)
    m = jnp.max(s, axis=-1, keepdims=True)
    p = jnp.exp(s - m)
    l = jnp.sum(p, axis=-1, keepdims=True)
    o = _dot(p.astype(BF16), v_ref[...])
    o_ref[...] = (o * (1.0 / l)).astype(o_ref.dtype)


def _attention(q, k, v):
    b, hq, s, _ = q.shape
    tq = _tile(s, 256)
    return pl.pallas_call(
        _attn_kernel,
        grid=(b, hq, s // tq),
        in_specs=[
            pl.BlockSpec((None, None, tq, QK_DIM), lambda bi, hi, qi: (bi, hi, qi, 0)),
            pl.BlockSpec((None, None, s, QK_DIM), lambda bi, hi, qi: (bi, hi, 0, 0)),
            pl.BlockSpec((None, None, s, V_HEAD), lambda bi, hi, qi: (bi, hi, 0, 0)),
        ],
        out_specs=pl.BlockSpec((None, tq, V_HEAD), lambda bi, hi, qi: (bi, qi, hi)),
        out_shape=jax.ShapeDtypeStruct((b, s, hq * V_HEAD), BF16),
        compiler_params=_params(("parallel", "parallel", "parallel")),
        name="attention",
    )(q, k, v)


def _attn_out_kernel(o_ref, wo_ref, x_ref, mod_ref, lng_ref, lnb_ref, xo_ref, ho_ref, *, alpha):
    y = _dot(o_ref[...], wo_ref[...])
    xn = _residual_layernorm(x_ref[...], y, mod_ref[2:3, :], lng_ref[...], lnb_ref[...], alpha)
    xo_ref[...] = xn
    ho_ref[...] = (xn * (1.0 + mod_ref[4:5, :]) + mod_ref[3:4, :]).astype(ho_ref.dtype)


def _attn_out(o, w_o, x, mod, ln_g, ln_b, alpha, seq):
    t, d = x.shape
    tm = _tile(seq, 512)
    per_b = seq // tm
    wo = w_o.astype(BF16)
    return pl.pallas_call(
        functools.partial(_attn_out_kernel, alpha=alpha),
        grid=(t // tm,),
        in_specs=[
            pl.BlockSpec((tm, o.shape[1]), lambda i: (i, 0)),
            pl.BlockSpec(wo.shape, lambda i: (0, 0)),
            pl.BlockSpec((tm, d), lambda i: (i, 0)),
            pl.BlockSpec((None, N_MOD, d), lambda i: (i // per_b, 0, 0)),
            pl.BlockSpec((1, d), lambda i: (0, 0)),
            pl.BlockSpec((1, d), lambda i: (0, 0)),
        ],
        out_specs=[pl.BlockSpec((tm, d), lambda i: (i, 0))] * 2,
        out_shape=[jax.ShapeDtypeStruct((t, d), F32), jax.ShapeDtypeStruct((t, d), BF16)],
        compiler_params=_params(("parallel",)),
        name="attn_out_ln",
    )(o, wo, x, mod, ln_g.reshape(1, d), ln_b.reshape(1, d))


def _swiglu_step(h, wg_ref, wu_ref, wd_ref, acc_ref):
    g = _dot(h, wg_ref[...])
    u = _dot(h, wu_ref[...])
    a = (_silu(g) * u).astype(BF16)
    acc_ref[...] += _dot(a, wd_ref[...])


def _ffn_dense_kernel(h_ref, wg_ref, wu_ref, wd_ref, x_ref, mod_ref, lng_ref, lnb_ref, o_ref, acc_ref, *, alpha):
    f = pl.program_id(1)

    @pl.when(f == 0)
    def _():
        acc_ref[...] = jnp.zeros_like(acc_ref)

    _swiglu_step(h_ref[...], wg_ref, wu_ref, wd_ref, acc_ref)

    @pl.when(f == pl.num_programs(1) - 1)
    def _():
        o_ref[...] = _residual_layernorm(x_ref[...], acc_ref[...], mod_ref[5:6, :], lng_ref[...], lnb_ref[...], alpha)


def _ffn_dense(h, wg, wu, wd, x, mod, ln_g, ln_b, alpha, seq):
    t, d = x.shape
    ff = wg.shape[1]
    tm = _tile(seq, 512)
    tf = _tile(ff, 512)
    per_b = seq // tm
    return pl.pallas_call(
        functools.partial(_ffn_dense_kernel, alpha=alpha),
        grid=(t // tm, ff // tf),
        in_specs=[
            pl.BlockSpec((tm, d), lambda i, f: (i, 0)),
            pl.BlockSpec((d, tf), lambda i, f: (0, f)),
            pl.BlockSpec((d, tf), lambda i, f: (0, f)),
            pl.BlockSpec((tf, d), lambda i, f: (f, 0)),
            pl.BlockSpec((tm, d), lambda i, f: (i, 0)),
            pl.BlockSpec((None, N_MOD, d), lambda i, f: (i // per_b, 0, 0)),
            pl.BlockSpec((1, d), lambda i, f: (0, 0)),
            pl.BlockSpec((1, d), lambda i, f: (0, 0)),
        ],
        out_specs=pl.BlockSpec((tm, d), lambda i, f: (i, 0)),
        out_shape=jax.ShapeDtypeStruct((t, d), F32),
        scratch_shapes=[pltpu.VMEM((tm, d), F32)],
        compiler_params=_params(("parallel", "arbitrary")),
        name="ffn_dense_ln",
    )(h, wg.astype(BF16), wu.astype(BF16), wd.astype(BF16), x, mod, ln_g.reshape(1, d), ln_b.reshape(1, d))


def _pack_bf16_pairs(v):
    n = v.shape[1] // 2
    lo = pltpu.bitcast(v[:, :n].astype(BF16).astype(F32), U32)
    hi = pltpu.bitcast(v[:, n:].astype(BF16).astype(F32), U32)
    return hi | (lo >> 16)


def _unpack_bf16_pairs(w):
    lo = pltpu.bitcast(w << 16, F32).astype(BF16)
    hi = pltpu.bitcast(w & jnp.uint32(0xFFFF0000), F32).astype(BF16)
    return jnp.concatenate([lo, hi], axis=1)


def _pool_kernel(x_ref, xp_ref, xn_ref, mod_ref, pw_ref, ps_ref, lng_ref, lnb_ref, rt_ref,
                 xo_ref, hp_ref, route_ref, hbuf_ref, *, alpha, seq, n_exp):
    tm, d = x_ref.shape
    gdim = d // len(POOL_WINDOWS)
    sh, sc = mod_ref[0:1, :], mod_ref[1:2, :]
    s0 = (pl.program_id(0) * tm) % seq
    x = x_ref[...]
    hc = x * (1.0 + sc) + sh
    has_prev = (s0 > 0).astype(F32)
    has_next = (s0 + tm < seq).astype(F32)
    hbuf_ref[0:POOL_HALO, :] = (xp_ref[...] * (1.0 + sc) + sh) * has_prev
    hbuf_ref[POOL_HALO:POOL_HALO + tm, :] = hc
    hbuf_ref[POOL_HALO + tm:, :] = (xn_ref[...] * (1.0 + sc) + sh) * has_next
    pos = s0 + lax.broadcasted_iota(I32, (tm, 1), 0)
    ys = []
    for g, w in enumerate(POOL_WINDOWS):
        left = w // 2
        right = w - 1 - left
        cols = slice(g * gdim, (g + 1) * gdim)
        win = hbuf_ref[POOL_HALO - left:POOL_HALO - left + tm, cols]
        for k in range(-left + 1, right + 1):
            win = win + hbuf_ref[POOL_HALO + k:POOL_HALO + k + tm, cols]
        cnt = (jnp.minimum(pos + right, seq - 1) - jnp.maximum(pos - left, 0) + 1).astype(F32)
        dg = (win / cnt - hc[:, cols]).astype(BF16)
        ys.append(_dot(dg, pw_ref[g]))
    y = jnp.concatenate(ys, axis=1) * ps_ref[...]
    xn = _residual_layernorm(x, y, mod_ref[2:3, :], lng_ref[...], lnb_ref[...], alpha)
    xo_ref[...] = xn
    h2 = xn * (1.0 + mod_ref[4:5, :]) + mod_ref[3:4, :]
    hp_ref[...] = _pack_bf16_pairs(h2)
    logits = _dot(h2, rt_ref[...], precision=lax.Precision.HIGHEST)
    lane = lax.broadcasted_iota(I32, logits.shape, 1)
    logits = jnp.where(lane < n_exp, logits, -jnp.inf)
    ex = jnp.exp(logits - jnp.max(logits, axis=-1, keepdims=True))
    probs = ex / jnp.sum(ex, axis=-1, keepdims=True)
    p1 = jnp.max(probs, axis=-1, keepdims=True)
    i1 = jnp.min(jnp.where(probs == p1, lane, LANES), axis=-1, keepdims=True)
    rest = jnp.where(lane == i1, -1.0, probs)
    p2 = jnp.max(rest, axis=-1, keepdims=True)
    i2 = jnp.min(jnp.where(rest == p2, lane, LANES), axis=-1, keepdims=True)
    den = p1 + p2
    route = jnp.where(lane == 0, i1.astype(F32), jnp.where(lane == 1, i2.astype(F32),
                      jnp.where(lane == 2, p1 / den, jnp.where(lane == 3, p2 / den, 0.0))))
    route_ref[...] = route


def _pool_layer(x, mod, pool_w, pool_scale, ln_g, ln_b, router, alpha, seq):
    t, d = x.shape
    n_exp = router.shape[1]
    tm = _tile(seq, 256)
    per_halo = tm // POOL_HALO
    n_halo = t // POOL_HALO
    rt = jnp.zeros((d, LANES), F32).at[:, :n_exp].set(router)
    pw = pool_w.astype(BF16)
    return pl.pallas_call(
        functools.partial(_pool_kernel, alpha=alpha, seq=seq, n_exp=n_exp),
        grid=(t // tm,),
        in_specs=[
            pl.BlockSpec((tm, d), lambda i: (i, 0)),
            pl.BlockSpec((POOL_HALO, d), lambda i: (jnp.maximum(i * per_halo - 1, 0), 0)),
            pl.BlockSpec((POOL_HALO, d), lambda i: (jnp.minimum((i + 1) * per_halo, n_halo - 1), 0)),
            pl.BlockSpec((None, N_MOD, d), lambda i: (i * tm // seq, 0, 0)),
            pl.BlockSpec(pw.shape, lambda i: (0, 0, 0)),
            pl.BlockSpec((1, d), lambda i: (0, 0)),
            pl.BlockSpec((1, d), lambda i: (0, 0)),
            pl.BlockSpec((1, d), lambda i: (0, 0)),
            pl.BlockSpec((d, LANES), lambda i: (0, 0)),
        ],
        out_specs=[
            pl.BlockSpec((tm, d), lambda i: (i, 0)),
            pl.BlockSpec((tm, d // 2), lambda i: (i, 0)),
            pl.BlockSpec((tm, LANES), lambda i: (i, 0)),
        ],
        out_shape=[
            jax.ShapeDtypeStruct((t, d), F32),
            jax.ShapeDtypeStruct((t, d // 2), U32),
            jax.ShapeDtypeStruct((t, LANES), F32),
        ],
        scratch_shapes=[pltpu.VMEM((tm + 2 * POOL_HALO, d), F32)],
        compiler_params=_params(("parallel",)),
        name="pool_ln_router",
    )(x, x, x, mod, pw, pool_scale.reshape(1, d), ln_g.reshape(1, d), ln_b.reshape(1, d), rt)


def _route_tables(ids, n_exp, tm, n_tiles):
    t, k = ids.shape
    onehot = (ids[:, :, None] == jnp.arange(n_exp, dtype=I32)[None, None, :]).astype(I32).sum(axis=1)
    csum = jnp.cumsum(onehot, axis=0)
    rank = csum - onehot
    counts = csum[-1]
    padded = ((counts + tm - 1) // tm) * tm
    ends = jnp.cumsum(padded)
    offs = ends - padded
    slot = offs[ids] + jnp.take_along_axis(rank, ids, axis=1)
    tile_start = jnp.arange(n_tiles, dtype=I32) * tm
    tile_expert = jnp.minimum(jnp.sum(tile_start[:, None] >= ends[None, :], axis=1), n_exp - 1).astype(I32)
    n_valid = (ends[-1:] // tm).astype(I32)
    tok = jnp.zeros((n_tiles * tm,), I32).at[slot.reshape(-1)].set(jnp.repeat(jnp.arange(t, dtype=I32), k))
    return slot.astype(I32), tok, tile_expert, n_valid


def _gather_kernel(tok_ref, src_ref, o_ref, sem):
    tg = o_ref.shape[0]
    base = pl.program_id(0) * tg

    def row_copy(r, src_row):
        return pltpu.make_async_copy(src_ref.at[pl.ds(src_row, 1)], o_ref.at[pl.ds(r, 1)], sem)

    def start(r, carry):
        row_copy(r, tok_ref[base + r]).start()
        return carry

    def wait(r, carry):
        row_copy(r, 0).wait()
        return carry

    lax.fori_loop(0, tg, start, 0)
    lax.fori_loop(0, tg, wait, 0)


def _gather_rows(src, tok, tg):
    p = tok.shape[0]
    w = src.shape[1]
    return pl.pallas_call(
        _gather_kernel,
        grid_spec=pltpu.PrefetchScalarGridSpec(
            num_scalar_prefetch=1,
            grid=(p // tg,),
            in_specs=[pl.BlockSpec(memory_space=pl.ANY)],
            out_specs=pl.BlockSpec((tg, w), lambda i, tok: (i, 0)),
            scratch_shapes=[pltpu.SemaphoreType.DMA(())],
        ),
        out_shape=jax.ShapeDtypeStruct((p, w), src.dtype),
        compiler_params=_params(("arbitrary",)),
        name="moe_gather",
    )(tok, src)


def _ffn_group_kernel(te_ref, nv_ref, xp_ref, wg_ref, wu_ref, wd_ref, o_ref, xb_ref, acc_ref):
    i, f = pl.program_id(0), pl.program_id(1)
    valid = i < nv_ref[0]
    last = f == pl.num_programs(1) - 1

    @pl.when(valid & (f == 0))
    def _():
        xb_ref[...] = _unpack_bf16_pairs(xp_ref[...])
        acc_ref[...] = jnp.zeros_like(acc_ref)

    @pl.when(valid)
    def _():
        _swiglu_step(xb_ref[...], wg_ref, wu_ref, wd_ref, acc_ref)

    @pl.when(valid & last)
    def _():
        o_ref[...] = acc_ref[...]

    @pl.when(jnp.logical_not(valid) & last)
    def _():
        o_ref[...] = jnp.zeros_like(o_ref)


def _ffn_grouped(xs, tile_expert, n_valid, wg, wu, wd, tm):
    p, half = xs.shape
    d = 2 * half
    ff = wg.shape[2]
    tf = _tile(ff, 512)
    nf = ff // tf
    f_idx = lambda i, f, nv: jnp.where(i < nv[0], f, nf - 1)
    return pl.pallas_call(
        _ffn_group_kernel,
        grid_spec=pltpu.PrefetchScalarGridSpec(
            num_scalar_prefetch=2,
            grid=(p // tm, nf),
            in_specs=[
                pl.BlockSpec((tm, half), lambda i, f, te, nv: (i, 0)),
                pl.BlockSpec((None, d, tf), lambda i, f, te, nv: (te[i], 0, f_idx(i, f, nv))),
                pl.BlockSpec((None, d, tf), lambda i, f, te, nv: (te[i], 0, f_idx(i, f, nv))),
                pl.BlockSpec((None, tf, d), lambda i, f, te, nv: (te[i], f_idx(i, f, nv), 0)),
            ],
            out_specs=pl.BlockSpec((tm, d), lambda i, f, te, nv: (i, 0)),
            scratch_shapes=[pltpu.VMEM((tm, d), BF16), pltpu.VMEM((tm, d), F32)],
        ),
        out_shape=jax.ShapeDtypeStruct((p, d), F32),
        compiler_params=_params(("arbitrary", "arbitrary")),
        name="moe_ffn_grouped",
    )(tile_expert, n_valid, xs, wg.astype(BF16), wu.astype(BF16), wd.astype(BF16))


def _combine_kernel(slot_ref, ys_ref, route_ref, x_ref, mod_ref, lng_ref, lnb_ref, o_ref, buf_ref, sem, *, alpha):
    tm = x_ref.shape[0]
    base = pl.program_id(0) * tm * TOP_K

    def row_copy(r, k, src_row):
        return pltpu.make_async_copy(ys_ref.at[pl.ds(src_row, 1)], buf_ref.at[k, pl.ds(r, 1)], sem)

    def start(r, carry):
        for k in range(TOP_K):
            row_copy(r, k, slot_ref[base + r * TOP_K + k]).start()
        return carry

    def wait(r, carry):
        for k in range(TOP_K):
            row_copy(r, k, 0).wait()
        return carry

    lax.fori_loop(0, tm, start, 0)
    lax.fori_loop(0, tm, wait, 0)
    route = route_ref[...]
    y = buf_ref[0] * route[:, TOP_K:TOP_K + 1]
    for k in range(1, TOP_K):
        y = y + buf_ref[k] * route[:, TOP_K + k:TOP_K + k + 1]
    o_ref[...] = _residual_layernorm(x_ref[...], y, mod_ref[5:6, :], lng_ref[...], lnb_ref[...], alpha)


def _moe_combine(ys, slot, route, x, mod, ln_g, ln_b, alpha, seq):
    t, d = x.shape
    tm = _tile(seq, 256)
    per_b = seq // tm
    return pl.pallas_call(
        functools.partial(_combine_kernel, alpha=alpha),
        grid_spec=pltpu.PrefetchScalarGridSpec(
            num_scalar_prefetch=1,
            grid=(t // tm,),
            in_specs=[
                pl.BlockSpec(memory_space=pl.ANY),
                pl.BlockSpec((tm, LANES), lambda i, s: (i, 0)),
                pl.BlockSpec((tm, d), lambda i, s: (i, 0)),
                pl.BlockSpec((None, N_MOD, d), lambda i, s: (i // per_b, 0, 0)),
                pl.BlockSpec((1, d), lambda i, s: (0, 0)),
                pl.BlockSpec((1, d), lambda i, s: (0, 0)),
            ],
            out_specs=pl.BlockSpec((tm, d), lambda i, s: (i, 0)),
            scratch_shapes=[pltpu.VMEM((TOP_K, tm, d), F32), pltpu.SemaphoreType.DMA(())],
        ),
        out_shape=jax.ShapeDtypeStruct((t, d), F32),
        compiler_params=_params(("arbitrary",)),
        name="moe_combine_ln",
    )(slot.reshape(-1), ys, route, x, mod, ln_g.reshape(1, d), ln_b.reshape(1, d))


def _moe_ffn(hp, route, x, mod, ln_g, ln_b, wg, wu, wd, alpha, seq):
    t = x.shape[0]
    n_exp = wg.shape[0]
    tm = _tile(t, 512)
    n_tiles = (t * TOP_K) // tm + n_exp
    ids = route[:, :TOP_K].astype(I32)
    slot, tok, tile_expert, n_valid = _route_tables(ids, n_exp, tm, n_tiles)
    xs = _gather_rows(hp, tok, tm)
    ys = _ffn_grouped(xs, tile_expert, n_valid, wg, wu, wd, tm)
    return _moe_combine(ys, slot, route, x, mod, ln_g, ln_b, alpha, seq)


def kernel(x, c, positions, mod_w, mod_b, ln_g, ln_b, mla_w_in, mla_q_norm, mla_w_uq, mla_kv_norm, mla_w_ukv, mla_w_o, pool_w, pool_scale, ffn_w_gate, ffn_w_up, ffn_w_down, moe_router, moe_w_gate, moe_w_up, moe_w_down):
    b, s, d = x.shape
    depth = mod_w.shape[0]
    alpha = (2.0 * depth) ** 0.25
    mod = _modulation(c, mod_w, mod_b)
    cos, sin = _rope_tables(positions)
    xt = x.reshape(b * s, d)
    for i in range(depth):
        j = i // 2
        if i % 2 == 0:
            q, k, v = _mla_project(xt, mod[i], cos, sin, mla_w_in[j], mla_q_norm[j], mla_w_uq[j],
                                   mla_kv_norm[j], mla_w_ukv[j], b, s)
            o = _attention(q, k, v).reshape(b * s, MLA_HEADS * V_HEAD)
            xt, h2 = _attn_out(o, mla_w_o[j], xt, mod[i], ln_g[i, 0], ln_b[i, 0], alpha, s)
            xt = _ffn_dense(h2, ffn_w_gate[j], ffn_w_up[j], ffn_w_down[j], xt, mod[i], ln_g[i, 1], ln_b[i, 1], alpha, s)
        else:
            xt, hp, route = _pool_layer(xt, mod[i], pool_w[j], pool_scale[j], ln_g[i, 0], ln_b[i, 0],
                                        moe_router[j], alpha, s)
            xt = _moe_ffn(hp, route, xt, mod[i], ln_g[i, 1], ln_b[i, 1],
                          moe_w_gate[j], moe_w_up[j], moe_w_down[j], alpha, s)
    return xt.reshape(b, s, d)
```

```python
import functools
import math

import jax
import jax.numpy as jnp
from jax import lax
from jax.experimental import pallas as pl
from jax.experimental.pallas import tpu as pltpu

F32 = jnp.float32
BF16 = jnp.bfloat16
U32 = jnp.uint32
I32 = jnp.int32

MLA_HEADS = 16
QK_NOPE = 128
QK_ROPE = 64
V_HEAD = 128
Q_LORA = 512
KV_LORA = 256
ROPE_THETA = 10000.0
POOL_WINDOWS = (2, 4, 8, 16)
TOP_K = 2
LN_EPS = 1e-5
RMS_EPS = 1e-6
N_MOD = 6

QK_DIM = QK_NOPE + QK_ROPE
POOL_HALO = 8
LANES = 128
VMEM_LIMIT = 56 * 1024 * 1024


def _tile(n, pref):
    t = min(n, pref)
    assert n % t == 0, (n, pref)
    return t


def _params(sem, vmem=VMEM_LIMIT):
    return pltpu.CompilerParams(dimension_semantics=sem, vmem_limit_bytes=vmem)


def _dot(a, b, **kw):
    return jnp.dot(a, b, preferred_element_type=F32, **kw)


def _silu(v):
    return v * jax.nn.sigmoid(v)


def _residual_layernorm(x, y, gate, ln_g, ln_b, alpha):
    z = alpha * x + (1.0 + gate) * y
    mu = jnp.mean(z, axis=-1, keepdims=True)
    zc = z - mu
    var = jnp.mean(zc * zc, axis=-1, keepdims=True)
    return zc * lax.rsqrt(var + LN_EPS) * ln_g + ln_b


def _rmsnorm(v, g):
    return v * lax.rsqrt(jnp.mean(v * v, axis=-1, keepdims=True) + RMS_EPS) * g


def _mod_kernel(c_ref, w_ref, b_ref, o_ref):
    ca = _silu(c_ref[...])
    o_ref[...] = _dot(ca, w_ref[...], precision=lax.Precision.HIGHEST) + b_ref[...]


def _modulation(c, mod_w, mod_b):
    depth, d, nd = mod_w.shape
    b = c.shape[0]
    rows = 8
    assert b <= rows
    c_pad = jnp.zeros((rows, d), F32).at[:b].set(c)
    tn = _tile(nd, 1024)
    out = pl.pallas_call(
        _mod_kernel,
        grid=(depth, nd // tn),
        in_specs=[
            pl.BlockSpec((rows, d), lambda i, n: (0, 0)),
            pl.BlockSpec((None, d, tn), lambda i, n: (i, 0, n)),
            pl.BlockSpec((None, 1, tn), lambda i, n: (i, 0, n)),
        ],
        out_specs=pl.BlockSpec((None, rows, tn), lambda i, n: (i, 0, n)),
        out_shape=jax.ShapeDtypeStruct((depth, rows, nd), F32),
        compiler_params=_params(("parallel", "parallel")),
        name="modulation",
    )(c_pad, mod_w, mod_b.reshape(depth, 1, nd))
    return out[:, :b].reshape(depth, b, N_MOD, d)


def _rope_kernel(pos_ref, freq_ref, cos_ref, sin_ref):
    ang = pos_ref[...] * freq_ref[...]
    cos_ref[...] = jnp.cos(ang)
    sin_ref[...] = jnp.sin(ang)


def _rope_tables(positions):
    t = positions.size
    half = QK_ROPE // 2
    inv_freq = ROPE_THETA ** (-jnp.arange(0, QK_ROPE, 2, dtype=F32) / QK_ROPE)
    freq = jnp.tile(inv_freq, LANES // half).reshape(1, LANES)
    pos = positions.astype(F32).reshape(t, 1)
    tm = _tile(t, 2048)
    return pl.pallas_call(
        _rope_kernel,
        grid=(t // tm,),
        in_specs=[pl.BlockSpec((tm, 1), lambda i: (i, 0)), pl.BlockSpec((1, LANES), lambda i: (0, 0))],
        out_specs=[pl.BlockSpec((tm, LANES), lambda i: (i, 0))] * 2,
        out_shape=[jax.ShapeDtypeStruct((t, LANES), F32)] * 2,
        compiler_params=_params(("parallel",)),
        name="rope_tables",
    )(pos, freq)


def _mla_proj_kernel(x_ref, mod_ref, cos_ref, sin_ref, win_ref, qn_ref, wqn_ref, wqr_ref, wqt_ref,
                     kvn_ref, wk_ref, wv_ref, q_ref, k_ref, v_ref, *, scale):
    sh, sc = mod_ref[0:1, :], mod_ref[1:2, :]
    h = (x_ref[...] * (1.0 + sc) + sh).astype(BF16)
    lat = _dot(h, win_ref[...])
    cos, sin = cos_ref[...], sin_ref[...]
    qn = _rmsnorm(lat[:, :Q_LORA], qn_ref[...]).astype(BF16)
    kn = _rmsnorm(lat[:, Q_LORA:Q_LORA + KV_LORA], kvn_ref[...]).astype(BF16)
    r0 = Q_LORA + KV_LORA
    k_rope = (lat[:, r0:r0 + QK_ROPE] * cos[:, :QK_ROPE]
              + lat[:, r0 + QK_ROPE:r0 + 2 * QK_ROPE] * sin[:, :QK_ROPE]).astype(BF16)
    q_nope = (_dot(qn, wqn_ref[...]) * scale).astype(BF16)
    q_r, q_t = _dot(qn, wqr_ref[...]), _dot(qn, wqt_ref[...])
    k_nope = _dot(kn, wk_ref[...]).astype(BF16)
    v_ref_val = _dot(kn, wv_ref[...]).astype(BF16)
    heads_per_vreg = LANES // QK_ROPE
    for hp in range(MLA_HEADS // heads_per_vreg):
        cols = slice(hp * LANES, (hp + 1) * LANES)
        roped = ((q_r[:, cols] * cos + q_t[:, cols] * sin) * scale).astype(BF16)
        for sub in range(heads_per_vreg):
            hd = hp * heads_per_vreg + sub
            q_ref[hd, :, :QK_NOPE] = q_nope[:, hd * QK_NOPE:(hd + 1) * QK_NOPE]
            q_ref[hd, :, QK_NOPE:] = roped[:, sub * QK_ROPE:(sub + 1) * QK_ROPE]
    for hd in range(MLA_HEADS):
        k_ref[hd, :, :QK_NOPE] = k_nope[:, hd * QK_NOPE:(hd + 1) * QK_NOPE]
        k_ref[hd, :, QK_NOPE:] = k_rope
        v_ref[hd, :, :V_HEAD] = v_ref_val[:, hd * V_HEAD:(hd + 1) * V_HEAD]
        v_ref[hd, :, V_HEAD:] = jnp.ones((v_ref.shape[1], V_HEAD), BF16)


def _rotate_half_cols(w):
    lead = w.shape[0]
    w = w.reshape(lead, -1, 2, QK_ROPE // 2)
    return jnp.concatenate([-w[:, :, 1:], w[:, :, :1]], axis=2).reshape(lead, -1)


def _mla_project(x, mod, cos, sin, w_in, q_norm, w_uq, kv_norm, w_ukv, batch, seq):
    t, d = x.shape
    hq = MLA_HEADS
    w_rope = w_in[:, Q_LORA + KV_LORA:]
    win = jnp.concatenate([w_in, _rotate_half_cols(w_rope)], axis=1).astype(BF16)
    wq = w_uq.reshape(Q_LORA, hq, QK_DIM)
    wq_nope = wq[:, :, :QK_NOPE].reshape(Q_LORA, hq * QK_NOPE).astype(BF16)
    wq_rope = wq[:, :, QK_NOPE:].reshape(Q_LORA, hq * QK_ROPE)
    wq_rot = _rotate_half_cols(wq_rope).astype(BF16)
    wq_rope = wq_rope.astype(BF16)
    wkv = w_ukv.reshape(KV_LORA, hq, QK_NOPE + V_HEAD)
    wk = wkv[:, :, :QK_NOPE].reshape(KV_LORA, hq * QK_NOPE).astype(BF16)
    wv = wkv[:, :, QK_NOPE:].reshape(KV_LORA, hq * V_HEAD).astype(BF16)
    tm = _tile(seq, 256)
    per_b = seq // tm
    full = lambda a: pl.BlockSpec(a.shape, lambda i: (0,) * a.ndim)
    qn2, kvn2 = q_norm.reshape(1, -1), kv_norm.reshape(1, -1)
    head_spec = lambda w: pl.BlockSpec((None, hq, tm, w), lambda i: (i // per_b, 0, i % per_b, 0))
    return pl.pallas_call(
        functools.partial(_mla_proj_kernel, scale=float(QK_DIM) ** -0.5 * math.log2(math.e)),
        grid=(t // tm,),
        in_specs=[
            pl.BlockSpec((tm, d), lambda i: (i, 0)),
            pl.BlockSpec((None, N_MOD, d), lambda i: (i // per_b, 0, 0)),
            pl.BlockSpec((tm, LANES), lambda i: (i, 0)),
            pl.BlockSpec((tm, LANES), lambda i: (i, 0)),
            full(win), full(qn2), full(wq_nope), full(wq_rope), full(wq_rot), full(kvn2), full(wk), full(wv),
        ],
        out_specs=[head_spec(QK_DIM), head_spec(QK_DIM), head_spec(2 * V_HEAD)],
        out_shape=[
            jax.ShapeDtypeStruct((batch, hq, seq, QK_DIM), BF16),
            jax.ShapeDtypeStruct((batch, hq, seq, QK_DIM), BF16),
            jax.ShapeDtypeStruct((batch, hq, seq, 2 * V_HEAD), BF16),
        ],
        compiler_params=_params(("parallel",)),
        name="mla_project",
    )(x, mod, cos, sin, win, qn2, wq_nope, wq_rope, wq_rot, kvn2, wk, wv)


ATTN_STAGES = 3


def _attn_kernel(q_ref, k_ref, v_ref, o_ref, s_ref, p_ref, m_ref, *, kc):
    rows, nk = s_ref.shape

    @pl.when(pl.program_id(0) == 0)
    def _():
        s_ref[...] = jnp.zeros_like(s_ref)
        p_ref[...] = jnp.ones_like(p_ref)
        m_ref[...] = jnp.zeros_like(m_ref)

    q = q_ref[...]
    m_prev = m_ref[...]
    m_run = None
    acc = None
    for j in range(nk // kc):
        keys = slice(j * kc, (j + 1) * kc)
        pv = _dot(p_ref[:, keys], v_ref[keys, :])
        acc = pv if acc is None else acc + pv
        for g in range(kc // LANES):
            cols = slice(j * kc + g * LANES, j * kc + (g + 1) * LANES)
            p_ref[:, cols] = jnp.exp2(s_ref[:, cols] - m_prev).astype(BF16)
        s = lax.dot_general(q, k_ref[keys, :], (((1,), (1,)), ((), ())), preferred_element_type=F32)
        s_ref[:, keys] = s
        for g in range(kc // LANES):
            sg = s[:, g * LANES:(g + 1) * LANES]
            m_run = sg if m_run is None else jnp.maximum(m_run, sg)
    o_ref[...] = (acc[:, :V_HEAD] * (1.0 / acc[:, V_HEAD:])).astype(o_ref.dtype)
    m_ref[...] = jnp.broadcast_to(jnp.max(m_run, axis=-1, keepdims=True), m_ref.shape)


def _attention(q, k, v):
    b, hq, s, _ = q.shape
    rows = _tile(s, 1024)
    per_head = s // rows
    n = b * hq * per_head
    lag = ATTN_STAGES - 1
    cur = lambda i: jnp.minimum(i, n - 1)
    old = lambda i: jnp.maximum(i - lag, 0)
    flat = lambda a: a.reshape(b * hq, s, a.shape[-1])
    return pl.pallas_call(
        functools.partial(_attn_kernel, kc=_tile(s, 512)),
        grid=(n + lag,),
        in_specs=[
            pl.BlockSpec((None, rows, QK_DIM), lambda i: (cur(i) // per_head, cur(i) % per_head, 0)),
            pl.BlockSpec((None, s, QK_DIM), lambda i: (cur(i) // per_head, 0, 0)),
            pl.BlockSpec((None, s, 2 * V_HEAD), lambda i: (old(i) // per_head, 0, 0)),
        ],
        out_specs=pl.BlockSpec((None, rows, V_HEAD),
                               lambda i: (old(i) // per_head // hq, old(i) % per_head, old(i) // per_head % hq)),
        out_shape=jax.ShapeDtypeStruct((b, s, hq * V_HEAD), BF16),
        scratch_shapes=[pltpu.VMEM((rows, s), F32), pltpu.VMEM((rows, s), BF16), pltpu.VMEM((rows, LANES), F32)],
        compiler_params=_params(("arbitrary",)),
        name="attention",
    )(flat(q), flat(k), flat(v))


def _attn_out_kernel(o_ref, wo_ref, x_ref, mod_ref, lng_ref, lnb_ref, xo_ref, ho_ref, *, alpha):
    y = _dot(o_ref[...], wo_ref[...])
    xn = _residual_layernorm(x_ref[...], y, mod_ref[2:3, :], lng_ref[...], lnb_ref[...], alpha)
    xo_ref[...] = xn
    ho_ref[...] = (xn * (1.0 + mod_ref[4:5, :]) + mod_ref[3:4, :]).astype(ho_ref.dtype)


def _attn_out(o, w_o, x, mod, ln_g, ln_b, alpha, seq):
    t, d = x.shape
    tm = _tile(seq, 512)
    per_b = seq // tm
    wo = w_o.astype(BF16)
    return pl.pallas_call(
        functools.partial(_attn_out_kernel, alpha=alpha),
        grid=(t // tm,),
        in_specs=[
            pl.BlockSpec((tm, o.shape[1]), lambda i: (i, 0)),
            pl.BlockSpec(wo.shape, lambda i: (0, 0)),
            pl.BlockSpec((tm, d), lambda i: (i, 0)),
            pl.BlockSpec((None, N_MOD, d), lambda i: (i // per_b, 0, 0)),
            pl.BlockSpec((1, d), lambda i: (0, 0)),
            pl.BlockSpec((1, d), lambda i: (0, 0)),
        ],
        out_specs=[pl.BlockSpec((tm, d), lambda i: (i, 0))] * 2,
        out_shape=[jax.ShapeDtypeStruct((t, d), F32), jax.ShapeDtypeStruct((t, d), BF16)],
        compiler_params=_params(("parallel",)),
        name="attn_out_ln",
    )(o, wo, x, mod, ln_g.reshape(1, d), ln_b.reshape(1, d))


def _swiglu_step(h, wg_ref, wu_ref, wd_ref, acc_ref):
    g = _dot(h, wg_ref[...])
    u = _dot(h, wu_ref[...])
    a = (_silu(g) * u).astype(BF16)
    acc_ref[...] += _dot(a, wd_ref[...])


def _ffn_dense_kernel(h_ref, wg_ref, wu_ref, wd_ref, x_ref, mod_ref, lng_ref, lnb_ref, o_ref, acc_ref, *, alpha):
    f = pl.program_id(1)

    @pl.when(f == 0)
    def _():
        acc_ref[...] = jnp.zeros_like(acc_ref)

    _swiglu_step(h_ref[...], wg_ref, wu_ref, wd_ref, acc_ref)

    @pl.when(f == pl.num_programs(1) - 1)
    def _():
        o_ref[...] = _residual_layernorm(x_ref[...], acc_ref[...], mod_ref[5:6, :], lng_ref[...], lnb_ref[...], alpha)


def _ffn_dense(h, wg, wu, wd, x, mod, ln_g, ln_b, alpha, seq):
    t, d = x.shape
    ff = wg.shape[1]
    tm = _tile(seq, 512)
    tf = _tile(ff, 512)
    per_b = seq // tm
    return pl.pallas_call(
        functools.partial(_ffn_dense_kernel, alpha=alpha),
        grid=(t // tm, ff // tf),
        in_specs=[
            pl.BlockSpec((tm, d), lambda i, f: (i, 0)),
            pl.BlockSpec((d, tf), lambda i, f: (0, f)),
            pl.BlockSpec((d, tf), lambda i, f: (0, f)),
            pl.BlockSpec((tf, d), lambda i, f: (f, 0)),
            pl.BlockSpec((tm, d), lambda i, f: (i, 0)),
            pl.BlockSpec((None, N_MOD, d), lambda i, f: (i // per_b, 0, 0)),
            pl.BlockSpec((1, d), lambda i, f: (0, 0)),
            pl.BlockSpec((1, d), lambda i, f: (0, 0)),
        ],
        out_specs=pl.BlockSpec((tm, d), lambda i, f: (i, 0)),
        out_shape=jax.ShapeDtypeStruct((t, d), F32),
        scratch_shapes=[pltpu.VMEM((tm, d), F32)],
        compiler_params=_params(("parallel", "arbitrary")),
        name="ffn_dense_ln",
    )(h, wg.astype(BF16), wu.astype(BF16), wd.astype(BF16), x, mod, ln_g.reshape(1, d), ln_b.reshape(1, d))


def _pack_bf16_pairs(v):
    n = v.shape[1] // 2
    lo = pltpu.bitcast(v[:, :n].astype(BF16).astype(F32), U32)
    hi = pltpu.bitcast(v[:, n:].astype(BF16).astype(F32), U32)
    return hi | (lo >> 16)


def _unpack_bf16_pairs(w):
    lo = pltpu.bitcast(w << 16, F32).astype(BF16)
    hi = pltpu.bitcast(w & jnp.uint32(0xFFFF0000), F32).astype(BF16)
    return jnp.concatenate([lo, hi], axis=1)


def _centred_window_sum(h, w):
    assert w >= 2 and w & (w - 1) == 0, w
    n = h.shape[0]
    f, m = h, 1
    while m < w // 2:
        f = f + pltpu.roll(f, n - m, axis=0)
        m *= 2
    return f + pltpu.roll(f, w // 2, axis=0)


def _pool_kernel(x_ref, xp_ref, xn_ref, mod_ref, pw_ref, ps_ref, lng_ref, lnb_ref, rt_ref,
                 xo_ref, hp_ref, route_ref, *, alpha, seq, n_exp):
    tm, d = x_ref.shape
    gdim = d // len(POOL_WINDOWS)
    sh, sc = mod_ref[0:1, :], mod_ref[1:2, :]
    s0 = (pl.program_id(0) * tm) % seq
    x = x_ref[...]
    hc = x * (1.0 + sc) + sh
    has_prev = (s0 > 0).astype(F32)
    has_next = (s0 + tm < seq).astype(F32)
    hcat = jnp.concatenate([(xp_ref[...] * (1.0 + sc) + sh) * has_prev, hc,
                            (xn_ref[...] * (1.0 + sc) + sh) * has_next], axis=0)
    pos = s0 + lax.broadcasted_iota(I32, (tm, 1), 0)
    ys = []
    for g, w in enumerate(POOL_WINDOWS):
        left = w // 2
        right = w - 1 - left
        cols = slice(g * gdim, (g + 1) * gdim)
        win = _centred_window_sum(hcat[:, cols], w)[POOL_HALO:POOL_HALO + tm]
        cnt = (jnp.minimum(pos + right, seq - 1) - jnp.maximum(pos - left, 0) + 1).astype(F32)
        dg = (win / cnt - hc[:, cols]).astype(BF16)
        ys.append(_dot(dg, pw_ref[g]))
    y = jnp.concatenate(ys, axis=1) * ps_ref[...]
    xn = _residual_layernorm(x, y, mod_ref[2:3, :], lng_ref[...], lnb_ref[...], alpha)
    xo_ref[...] = xn
    h2 = xn * (1.0 + mod_ref[4:5, :]) + mod_ref[3:4, :]
    hp_ref[...] = _pack_bf16_pairs(h2)
    h_hi = h2.astype(BF16)
    h_lo = (h2 - h_hi.astype(F32)).astype(BF16)
    hi_terms = _dot(h_hi, rt_ref[...])
    logits = hi_terms[:, :LANES] + hi_terms[:, LANES:] + _dot(h_lo, rt_ref[:, :LANES])
    lane = lax.broadcasted_iota(I32, logits.shape, 1)
    logits = jnp.where(lane < n_exp, logits, -jnp.inf)
    ex = jnp.exp(logits - jnp.max(logits, axis=-1, keepdims=True))
    probs = ex / jnp.sum(ex, axis=-1, keepdims=True)
    p1 = jnp.max(probs, axis=-1, keepdims=True)
    i1 = jnp.min(jnp.where(probs == p1, lane, LANES), axis=-1, keepdims=True)
    rest = jnp.where(lane == i1, -1.0, probs)
    p2 = jnp.max(rest, axis=-1, keepdims=True)
    i2 = jnp.min(jnp.where(rest == p2, lane, LANES), axis=-1, keepdims=True)
    den = p1 + p2
    route = jnp.where(lane == 0, i1.astype(F32), jnp.where(lane == 1, i2.astype(F32),
                      jnp.where(lane == 2, p1 / den, jnp.where(lane == 3, p2 / den, 0.0))))
    route_ref[...] = route


def _pool_layer(x, mod, pool_w, pool_scale, ln_g, ln_b, router, alpha, seq):
    t, d = x.shape
    n_exp = router.shape[1]
    tm = _tile(seq, 256)
    per_halo = tm // POOL_HALO
    n_halo = t // POOL_HALO
    rt_hi = router.astype(BF16)
    rt_lo = (router - rt_hi.astype(F32)).astype(BF16)
    rt = (jnp.zeros((d, 2 * LANES), BF16).at[:, :n_exp].set(rt_hi).at[:, LANES:LANES + n_exp].set(rt_lo))
    pw = pool_w.astype(BF16)
    return pl.pallas_call(
        functools.partial(_pool_kernel, alpha=alpha, seq=seq, n_exp=n_exp),
        grid=(t // tm,),
        in_specs=[
            pl.BlockSpec((tm, d), lambda i: (i, 0)),
            pl.BlockSpec((POOL_HALO, d), lambda i: (jnp.maximum(i * per_halo - 1, 0), 0)),
            pl.BlockSpec((POOL_HALO, d), lambda i: (jnp.minimum((i + 1) * per_halo, n_halo - 1), 0)),
            pl.BlockSpec((None, N_MOD, d), lambda i: (i * tm // seq, 0, 0)),
            pl.BlockSpec(pw.shape, lambda i: (0, 0, 0)),
            pl.BlockSpec((1, d), lambda i: (0, 0)),
            pl.BlockSpec((1, d), lambda i: (0, 0)),
            pl.BlockSpec((1, d), lambda i: (0, 0)),
            pl.BlockSpec((d, 2 * LANES), lambda i: (0, 0)),
        ],
        out_specs=[
            pl.BlockSpec((tm, d), lambda i: (i, 0)),
            pl.BlockSpec((tm, d // 2), lambda i: (i, 0)),
            pl.BlockSpec((tm, LANES), lambda i: (i, 0)),
        ],
        out_shape=[
            jax.ShapeDtypeStruct((t, d), F32),
            jax.ShapeDtypeStruct((t, d // 2), U32),
            jax.ShapeDtypeStruct((t, LANES), F32),
        ],
        compiler_params=_params(("parallel",)),
        name="pool_ln_router",
    )(x, x, x, mod, pw, pool_scale.reshape(1, d), ln_g.reshape(1, d), ln_b.reshape(1, d), rt)


def _route_tables(ids, n_exp, tm, n_tiles):
    t, k = ids.shape
    onehot = (ids[:, :, None] == jnp.arange(n_exp, dtype=I32)[None, None, :]).astype(I32).sum(axis=1)
    csum = jnp.cumsum(onehot, axis=0)
    rank = csum - onehot
    counts = csum[-1]
    padded = ((counts + tm - 1) // tm) * tm
    ends = jnp.cumsum(padded)
    offs = ends - padded
    slot = offs[ids] + jnp.take_along_axis(rank, ids, axis=1)
    tile_start = jnp.arange(n_tiles, dtype=I32) * tm
    tile_expert = jnp.minimum(jnp.sum(tile_start[:, None] >= ends[None, :], axis=1), n_exp - 1).astype(I32)
    n_valid = (ends[-1:] // tm).astype(I32)
    tok = jnp.zeros((n_tiles * tm,), I32).at[slot.reshape(-1)].set(jnp.repeat(jnp.arange(t, dtype=I32), k))
    return slot.astype(I32), tok, tile_expert, n_valid


def _ffn_group_kernel(te_ref, nv_ref, tok_ref, hp_ref, wg_ref, wu_ref, wd_ref, o_ref,
                      xbuf_ref, xb_ref, acc_ref, sem, *, rows_per_step):
    i, f = pl.program_id(0), pl.program_id(1)
    nf = pl.num_programs(1)
    tm = xb_ref.shape[0]
    n_rows = nf * rows_per_step
    n_live = nv_ref[0]
    live = i < n_live
    first, last = f == 0, f == nf - 1
    buf = i % 2

    def row_copy(src_row, r, b):
        return pltpu.make_async_copy(hp_ref.at[pl.ds(src_row, 1)], xbuf_ref.at[b, pl.ds(r, 1)], sem.at[b])

    @pl.when(first & (i == 0))
    def _():
        def start(r, carry):
            row_copy(tok_ref[r], r, 0).start()
            return carry
        lax.fori_loop(0, n_rows, start, 0, unroll=8)

    @pl.when(first & (i <= n_live))
    def _():
        def wait(r, carry):
            row_copy(0, r, buf).wait()
            return carry
        lax.fori_loop(0, n_rows, wait, 0, unroll=8)

    @pl.when(first & live)
    def _():
        xb_ref[...] = _unpack_bf16_pairs(xbuf_ref[buf, :tm, :])
        acc_ref[...] = jnp.zeros_like(acc_ref)

    @pl.when(live)
    def _():
        for u in range(rows_per_step):
            r = f * rows_per_step + u
            row_copy(tok_ref[(i + 1) * tm + r], r, 1 - buf).start()
        _swiglu_step(xb_ref[...], wg_ref, wu_ref, wd_ref, acc_ref)

    @pl.when(live & last)
    def _():
        o_ref[...] = acc_ref[...]

    @pl.when(jnp.logical_not(live) & last)
    def _():
        o_ref[...] = jnp.zeros_like(o_ref)


def _ffn_grouped(hp, tok, tile_expert, n_valid, wg, wu, wd, tm, rows_per_step):
    half = hp.shape[1]
    d = 2 * half
    n_tiles = tile_expert.shape[0]
    ff = wg.shape[2]
    tf = _tile(ff, 512)
    nf = ff // tf
    n_rows = nf * rows_per_step
    f_idx = lambda i, f, nv: jnp.where(i < nv[0], f, nf - 1)
    return pl.pallas_call(
        functools.partial(_ffn_group_kernel, rows_per_step=rows_per_step),
        grid_spec=pltpu.PrefetchScalarGridSpec(
            num_scalar_prefetch=3,
            grid=(n_tiles, nf),
            in_specs=[
                pl.BlockSpec(memory_space=pl.ANY),
                pl.BlockSpec((None, d, tf), lambda i, f, te, nv, tok: (te[i], 0, f_idx(i, f, nv))),
                pl.BlockSpec((None, d, tf), lambda i, f, te, nv, tok: (te[i], 0, f_idx(i, f, nv))),
                pl.BlockSpec((None, tf, d), lambda i, f, te, nv, tok: (te[i], f_idx(i, f, nv), 0)),
            ],
            out_specs=pl.BlockSpec((tm, d), lambda i, f, te, nv, tok: (i, 0)),
            scratch_shapes=[
                pltpu.VMEM((2, pl.cdiv(n_rows, 8) * 8, half), U32),
                pltpu.VMEM((tm, d), BF16),
                pltpu.VMEM((tm, d), F32),
                pltpu.SemaphoreType.DMA((2,)),
            ],
        ),
        out_shape=jax.ShapeDtypeStruct((n_tiles * tm, d), F32),
        compiler_params=_params(("arbitrary", "arbitrary")),
        name="moe_ffn_grouped",
    )(tile_expert, n_valid, tok, hp, wg.astype(BF16), wu.astype(BF16), wd.astype(BF16))


def _combine_kernel(slot_ref, ys_ref, route_ref, x_ref, mod_ref, lng_ref, lnb_ref, o_ref, buf_ref, sem, *, alpha):
    tm = x_ref.shape[0]
    i = pl.program_id(0)
    cur = i % 2

    def row_copy(src_row, r, k, b):
        return pltpu.make_async_copy(ys_ref.at[pl.ds(src_row, 1)], buf_ref.at[b, k, pl.ds(r, 1)], sem.at[b])

    def wait_tile(b):
        def wait(r, carry):
            for k in range(TOP_K):
                row_copy(0, r, k, b).wait()
            return carry
        lax.fori_loop(0, tm, wait, 0, unroll=8)

    @pl.when(i == 0)
    def _():
        def start(r, carry):
            for k in range(TOP_K):
                row_copy(slot_ref[r * TOP_K + k], r, k, 0).start()
            return carry
        lax.fori_loop(0, tm, start, 0, unroll=8)

    wait_tile(cur)
    base = (i + 1) * tm * TOP_K
    for r in range(tm):
        for k in range(TOP_K):
            row_copy(slot_ref[base + r * TOP_K + k], r, k, 1 - cur).start()
    route = route_ref[...]
    y = buf_ref[cur, 0] * route[:, TOP_K:TOP_K + 1]
    for k in range(1, TOP_K):
        y = y + buf_ref[cur, k] * route[:, TOP_K + k:TOP_K + k + 1]
    o_ref[...] = _residual_layernorm(x_ref[...], y, mod_ref[5:6, :], lng_ref[...], lnb_ref[...], alpha)

    @pl.when(i == pl.num_programs(0) - 1)
    def _():
        wait_tile(1 - cur)


def _moe_combine(ys, slot, route, x, mod, ln_g, ln_b, alpha, seq):
    t, d = x.shape
    tm = _tile(seq, 256)
    per_b = seq // tm
    slot = jnp.concatenate([slot.reshape(-1), jnp.zeros((tm * TOP_K,), I32)])
    return pl.pallas_call(
        functools.partial(_combine_kernel, alpha=alpha),
        grid_spec=pltpu.PrefetchScalarGridSpec(
            num_scalar_prefetch=1,
            grid=(t // tm,),
            in_specs=[
                pl.BlockSpec(memory_space=pl.ANY),
                pl.BlockSpec((tm, LANES), lambda i, s: (i, 0)),
                pl.BlockSpec((tm, d), lambda i, s: (i, 0)),
                pl.BlockSpec((None, N_MOD, d), lambda i, s: (i // per_b, 0, 0)),
                pl.BlockSpec((1, d), lambda i, s: (0, 0)),
                pl.BlockSpec((1, d), lambda i, s: (0, 0)),
            ],
            out_specs=pl.BlockSpec((tm, d), lambda i, s: (i, 0)),
            scratch_shapes=[pltpu.VMEM((2, TOP_K, tm, d), F32), pltpu.SemaphoreType.DMA((2,))],
        ),
        out_shape=jax.ShapeDtypeStruct((t, d), F32),
        compiler_params=_params(("arbitrary",)),
        name="moe_combine_ln",
    )(slot, ys, route, x, mod, ln_g.reshape(1, d), ln_b.reshape(1, d))


def _moe_ffn(hp, route, x, mod, ln_g, ln_b, wg, wu, wd, alpha, seq):
    t = x.shape[0]
    n_exp, _, ff = wg.shape
    tm = _tile(t, 512)
    n_tiles = (t * TOP_K) // tm + n_exp
    rows_per_step = pl.cdiv(tm, ff // _tile(ff, 512))
    ids = route[:, :TOP_K].astype(I32)
    slot, tok, tile_expert, n_valid = _route_tables(ids, n_exp, tm, n_tiles)
    tok = jnp.concatenate([tok, jnp.zeros((tm,), I32)])
    ys = _ffn_grouped(hp, tok, tile_expert, n_valid, wg, wu, wd, tm, rows_per_step)
    return _moe_combine(ys, slot, route, x, mod, ln_g, ln_b, alpha, seq)


def kernel(x, c, positions, mod_w, mod_b, ln_g, ln_b, mla_w_in, mla_q_norm, mla_w_uq, mla_kv_norm, mla_w_ukv, mla_w_o, pool_w, pool_scale, ffn_w_gate, ffn_w_up, ffn_w_down, moe_router, moe_w_gate, moe_w_up, moe_w_down):
    b, s, d = x.shape
    depth = mod_w.shape[0]
    alpha = (2.0 * depth) ** 0.25
    mod = _modulation(c, mod_w, mod_b)
    cos, sin = _rope_tables(positions)
    xt = x.reshape(b * s, d)
    for i in range(depth):
        j = i // 2
        if i % 2 == 0:
            q, k, v = _mla_project(xt, mod[i], cos, sin, mla_w_in[j], mla_q_norm[j], mla_w_uq[j],
                                   mla_kv_norm[j], mla_w_ukv[j], b, s)
            o = _attention(q, k, v).reshape(b * s, MLA_HEADS * V_HEAD)
            xt, h2 = _attn_out(o, mla_w_o[j], xt, mod[i], ln_g[i, 0], ln_b[i, 0], alpha, s)
            xt = _ffn_dense(h2, ffn_w_gate[j], ffn_w_up[j], ffn_w_down[j], xt, mod[i], ln_g[i, 1], ln_b[i, 1], alpha, s)
        else:
            xt, hp, route = _pool_layer(xt, mod[i], pool_w[j], pool_scale[j], ln_g[i, 0], ln_b[i, 0],
                                        moe_router[j], alpha, s)
            xt = _moe_ffn(hp, route, xt, mod[i], ln_g[i, 1], ln_b[i, 1],
                          moe_w_gate[j], moe_w_up[j], moe_w_down[j], alpha, s)
    return xt.reshape(b, s, d)
```

```python
import functools
import math

import jax
import jax.numpy as jnp
from jax import lax
from jax.experimental import pallas as pl
from jax.experimental.pallas import tpu as pltpu

F32 = jnp.float32
BF16 = jnp.bfloat16
U32 = jnp.uint32
I32 = jnp.int32

MLA_HEADS = 16
QK_NOPE = 128
QK_ROPE = 64
V_HEAD = 128
Q_LORA = 512
KV_LORA = 256
ROPE_THETA = 10000.0
POOL_WINDOWS = (2, 4, 8, 16)
TOP_K = 2
LN_EPS = 1e-5
RMS_EPS = 1e-6
N_MOD = 6

QK_DIM = QK_NOPE + QK_ROPE
POOL_HALO = 8
LANES = 128
VMEM_LIMIT = 56 * 1024 * 1024


def _tile(n, pref):
    t = min(n, pref)
    assert n % t == 0, (n, pref)
    return t


def _params(sem, vmem=VMEM_LIMIT):
    return pltpu.CompilerParams(dimension_semantics=sem, vmem_limit_bytes=vmem)


def _dot(a, b, **kw):
    return jnp.dot(a, b, preferred_element_type=F32, **kw)


def _silu(v):
    return v * jax.nn.sigmoid(v)


def _residual_layernorm(x, y, gate, ln_g, ln_b, alpha):
    z = alpha * x + (1.0 + gate) * y
    mu = jnp.mean(z, axis=-1, keepdims=True)
    zc = z - mu
    var = jnp.mean(zc * zc, axis=-1, keepdims=True)
    return zc * lax.rsqrt(var + LN_EPS) * ln_g + ln_b


def _rmsnorm(v, g):
    return v * lax.rsqrt(jnp.mean(v * v, axis=-1, keepdims=True) + RMS_EPS) * g


def _mod_kernel(c_ref, w_ref, b_ref, o_ref):
    ca = _silu(c_ref[...])
    o_ref[...] = _dot(ca, w_ref[...], precision=lax.Precision.HIGHEST) + b_ref[...]


def _modulation(c, mod_w, mod_b):
    depth, d, nd = mod_w.shape
    b = c.shape[0]
    rows = 8
    assert b <= rows
    c_pad = jnp.zeros((rows, d), F32).at[:b].set(c)
    tn = _tile(nd, 1024)
    out = pl.pallas_call(
        _mod_kernel,
        grid=(depth, nd // tn),
        in_specs=[
            pl.BlockSpec((rows, d), lambda i, n: (0, 0)),
            pl.BlockSpec((None, d, tn), lambda i, n: (i, 0, n)),
            pl.BlockSpec((None, 1, tn), lambda i, n: (i, 0, n)),
        ],
        out_specs=pl.BlockSpec((None, rows, tn), lambda i, n: (i, 0, n)),
        out_shape=jax.ShapeDtypeStruct((depth, rows, nd), F32),
        compiler_params=_params(("parallel", "parallel")),
        name="modulation",
    )(c_pad, mod_w, mod_b.reshape(depth, 1, nd))
    return out[:, :b].reshape(depth, b, N_MOD, d)


def _rope_kernel(pos_ref, freq_ref, cos_ref, sin_ref):
    ang = pos_ref[...] * freq_ref[...]
    cos_ref[...] = jnp.cos(ang)
    sin_ref[...] = jnp.sin(ang)


def _rope_tables(positions):
    t = positions.size
    half = QK_ROPE // 2
    inv_freq = ROPE_THETA ** (-jnp.arange(0, QK_ROPE, 2, dtype=F32) / QK_ROPE)
    freq = jnp.tile(inv_freq, LANES // half).reshape(1, LANES)
    pos = positions.astype(F32).reshape(t, 1)
    tm = _tile(t, 2048)
    return pl.pallas_call(
        _rope_kernel,
        grid=(t // tm,),
        in_specs=[pl.BlockSpec((tm, 1), lambda i: (i, 0)), pl.BlockSpec((1, LANES), lambda i: (0, 0))],
        out_specs=[pl.BlockSpec((tm, LANES), lambda i: (i, 0))] * 2,
        out_shape=[jax.ShapeDtypeStruct((t, LANES), F32)] * 2,
        compiler_params=_params(("parallel",)),
        name="rope_tables",
    )(pos, freq)


def _mla_proj_kernel(x_ref, mod_ref, cos_ref, sin_ref, win_ref, qn_ref, wqn_ref, wqr_ref, wqt_ref,
                     kvn_ref, wk_ref, wv_ref, q_ref, k_ref, v_ref, *, scale):
    sh, sc = mod_ref[0:1, :], mod_ref[1:2, :]
    h = (x_ref[...] * (1.0 + sc) + sh).astype(BF16)
    lat = _dot(h, win_ref[...])
    cos, sin = cos_ref[...], sin_ref[...]
    qn = _rmsnorm(lat[:, :Q_LORA], qn_ref[...]).astype(BF16)
    kn = _rmsnorm(lat[:, Q_LORA:Q_LORA + KV_LORA], kvn_ref[...]).astype(BF16)
    r0 = Q_LORA + KV_LORA
    k_rope = (lat[:, r0:r0 + QK_ROPE] * cos[:, :QK_ROPE]
              + lat[:, r0 + QK_ROPE:r0 + 2 * QK_ROPE] * sin[:, :QK_ROPE]).astype(BF16)
    q_nope = (_dot(qn, wqn_ref[...]) * scale).astype(BF16)
    q_r, q_t = _dot(qn, wqr_ref[...]), _dot(qn, wqt_ref[...])
    k_nope = _dot(kn, wk_ref[...]).astype(BF16)
    v_ref_val = _dot(kn, wv_ref[...]).astype(BF16)
    heads_per_vreg = LANES // QK_ROPE
    for hp in range(MLA_HEADS // heads_per_vreg):
        cols = slice(hp * LANES, (hp + 1) * LANES)
        roped = ((q_r[:, cols] * cos + q_t[:, cols] * sin) * scale).astype(BF16)
        for sub in range(heads_per_vreg):
            hd = hp * heads_per_vreg + sub
            q_ref[hd, :, :QK_NOPE] = q_nope[:, hd * QK_NOPE:(hd + 1) * QK_NOPE]
            q_ref[hd, :, QK_NOPE:] = roped[:, sub * QK_ROPE:(sub + 1) * QK_ROPE]
    for hd in range(MLA_HEADS):
        k_ref[hd, :, :QK_NOPE] = k_nope[:, hd * QK_NOPE:(hd + 1) * QK_NOPE]
        k_ref[hd, :, QK_NOPE:] = k_rope
        v_ref[hd, :, :V_HEAD] = v_ref_val[:, hd * V_HEAD:(hd + 1) * V_HEAD]
        v_ref[hd, :, V_HEAD:] = jnp.ones((v_ref.shape[1], V_HEAD), BF16)


def _rotate_half_cols(w):
    lead = w.shape[0]
    w = w.reshape(lead, -1, 2, QK_ROPE // 2)
    return jnp.concatenate([-w[:, :, 1:], w[:, :, :1]], axis=2).reshape(lead, -1)


def _mla_project(x, mod, cos, sin, w_in, q_norm, w_uq, kv_norm, w_ukv, batch, seq):
    t, d = x.shape
    hq = MLA_HEADS
    w_rope = w_in[:, Q_LORA + KV_LORA:]
    win = jnp.concatenate([w_in, _rotate_half_cols(w_rope)], axis=1).astype(BF16)
    wq = w_uq.reshape(Q_LORA, hq, QK_DIM)
    wq_nope = wq[:, :, :QK_NOPE].reshape(Q_LORA, hq * QK_NOPE).astype(BF16)
    wq_rope = wq[:, :, QK_NOPE:].reshape(Q_LORA, hq * QK_ROPE)
    wq_rot = _rotate_half_cols(wq_rope).astype(BF16)
    wq_rope = wq_rope.astype(BF16)
    wkv = w_ukv.reshape(KV_LORA, hq, QK_NOPE + V_HEAD)
    wk = wkv[:, :, :QK_NOPE].reshape(KV_LORA, hq * QK_NOPE).astype(BF16)
    wv = wkv[:, :, QK_NOPE:].reshape(KV_LORA, hq * V_HEAD).astype(BF16)
    tm = _tile(seq, 256)
    per_b = seq // tm
    full = lambda a: pl.BlockSpec(a.shape, lambda i: (0,) * a.ndim)
    qn2, kvn2 = q_norm.reshape(1, -1), kv_norm.reshape(1, -1)
    head_spec = lambda w: pl.BlockSpec((None, hq, tm, w), lambda i: (i // per_b, 0, i % per_b, 0))
    return pl.pallas_call(
        functools.partial(_mla_proj_kernel, scale=float(QK_DIM) ** -0.5 * math.log2(math.e)),
        grid=(t // tm,),
        in_specs=[
            pl.BlockSpec((tm, d), lambda i: (i, 0)),
            pl.BlockSpec((None, N_MOD, d), lambda i: (i // per_b, 0, 0)),
            pl.BlockSpec((tm, LANES), lambda i: (i, 0)),
            pl.BlockSpec((tm, LANES), lambda i: (i, 0)),
            full(win), full(qn2), full(wq_nope), full(wq_rope), full(wq_rot), full(kvn2), full(wk), full(wv),
        ],
        out_specs=[head_spec(QK_DIM), head_spec(QK_DIM), head_spec(2 * V_HEAD)],
        out_shape=[
            jax.ShapeDtypeStruct((batch, hq, seq, QK_DIM), BF16),
            jax.ShapeDtypeStruct((batch, hq, seq, QK_DIM), BF16),
            jax.ShapeDtypeStruct((batch, hq, seq, 2 * V_HEAD), BF16),
        ],
        compiler_params=_params(("parallel",)),
        name="mla_project",
    )(x, mod, cos, sin, win, qn2, wq_nope, wq_rope, wq_rot, kvn2, wk, wv)


ATTN_STAGES = 3


def _attn_kernel(q_ref, k_ref, v_ref, o_ref, s_ref, p_ref, m_ref, *, kc):
    rows, nk = s_ref.shape

    @pl.when(pl.program_id(0) == 0)
    def _():
        s_ref[...] = jnp.zeros_like(s_ref)
        p_ref[...] = jnp.ones_like(p_ref)
        m_ref[...] = jnp.zeros_like(m_ref)

    q = q_ref[...]
    m_prev = m_ref[...]
    m_run = None
    acc = None
    for j in range(nk // kc):
        keys = slice(j * kc, (j + 1) * kc)
        pv = _dot(p_ref[:, keys], v_ref[keys, :])
        acc = pv if acc is None else acc + pv
        for g in range(kc // LANES):
            cols = slice(j * kc + g * LANES, j * kc + (g + 1) * LANES)
            p_ref[:, cols] = jnp.exp2(s_ref[:, cols] - m_prev).astype(BF16)
        s = lax.dot_general(q, k_ref[keys, :], (((1,), (1,)), ((), ())), preferred_element_type=F32)
        s_ref[:, keys] = s
        for g in range(kc // LANES):
            sg = s[:, g * LANES:(g + 1) * LANES]
            m_run = sg if m_run is None else jnp.maximum(m_run, sg)
    o_ref[...] = (acc[:, :V_HEAD] * (1.0 / acc[:, V_HEAD:])).astype(o_ref.dtype)
    m_ref[...] = jnp.broadcast_to(jnp.max(m_run, axis=-1, keepdims=True), m_ref.shape)


def _attention(q, k, v):
    b, hq, s, _ = q.shape
    rows = _tile(s, 1024)
    per_head = s // rows
    n = b * hq * per_head
    lag = ATTN_STAGES - 1
    cur = lambda i: jnp.minimum(i, n - 1)
    old = lambda i: jnp.maximum(i - lag, 0)
    flat = lambda a: a.reshape(b * hq, s, a.shape[-1])
    return pl.pallas_call(
        functools.partial(_attn_kernel, kc=_tile(s, 512)),
        grid=(n + lag,),
        in_specs=[
            pl.BlockSpec((None, rows, QK_DIM), lambda i: (cur(i) // per_head, cur(i) % per_head, 0)),
            pl.BlockSpec((None, s, QK_DIM), lambda i: (cur(i) // per_head, 0, 0)),
            pl.BlockSpec((None, s, 2 * V_HEAD), lambda i: (old(i) // per_head, 0, 0)),
        ],
        out_specs=pl.BlockSpec((None, rows, V_HEAD),
                               lambda i: (old(i) // per_head // hq, old(i) % per_head, old(i) // per_head % hq)),
        out_shape=jax.ShapeDtypeStruct((b, s, hq * V_HEAD), BF16),
        scratch_shapes=[pltpu.VMEM((rows, s), F32), pltpu.VMEM((rows, s), BF16), pltpu.VMEM((rows, LANES), F32)],
        compiler_params=_params(("arbitrary",)),
        name="attention",
    )(flat(q), flat(k), flat(v))


def _attn_out_kernel(o_ref, wo_ref, x_ref, mod_ref, lng_ref, lnb_ref, xo_ref, ho_ref, *, alpha):
    y = _dot(o_ref[...], wo_ref[...])
    xn = _residual_layernorm(x_ref[...], y, mod_ref[2:3, :], lng_ref[...], lnb_ref[...], alpha)
    xo_ref[...] = xn
    ho_ref[...] = (xn * (1.0 + mod_ref[4:5, :]) + mod_ref[3:4, :]).astype(ho_ref.dtype)


def _attn_out(o, w_o, x, mod, ln_g, ln_b, alpha, seq):
    t, d = x.shape
    tm = _tile(seq, 512)
    per_b = seq // tm
    wo = w_o.astype(BF16)
    return pl.pallas_call(
        functools.partial(_attn_out_kernel, alpha=alpha),
        grid=(t // tm,),
        in_specs=[
            pl.BlockSpec((tm, o.shape[1]), lambda i: (i, 0)),
            pl.BlockSpec(wo.shape, lambda i: (0, 0)),
            pl.BlockSpec((tm, d), lambda i: (i, 0)),
            pl.BlockSpec((None, N_MOD, d), lambda i: (i // per_b, 0, 0)),
            pl.BlockSpec((1, d), lambda i: (0, 0)),
            pl.BlockSpec((1, d), lambda i: (0, 0)),
        ],
        out_specs=[pl.BlockSpec((tm, d), lambda i: (i, 0))] * 2,
        out_shape=[jax.ShapeDtypeStruct((t, d), F32), jax.ShapeDtypeStruct((t, d), BF16)],
        compiler_params=_params(("parallel",)),
        name="attn_out_ln",
    )(o, wo, x, mod, ln_g.reshape(1, d), ln_b.reshape(1, d))


FFN_ROWS = 1024
FFN_COLS = 256


def _swiglu_step(h, wg, wu, wd, acc_ref):
    g = _dot(h, wg)
    u = _dot(h, wu)
    a = (_silu(g) * u).astype(BF16)
    acc_ref[...] += _dot(a, wd)


def _ffn_dense_kernel(h_ref, wg_ref, wu_ref, wd_ref, x_ref, mod_ref, lng_ref, lnb_ref, o_ref, *, alpha):
    f = pl.program_id(1)

    @pl.when(f == 0)
    def _():
        o_ref[...] = jnp.zeros_like(o_ref)

    _swiglu_step(h_ref[...], wg_ref[...].astype(BF16), wu_ref[...].astype(BF16), wd_ref[...].astype(BF16), o_ref)

    @pl.when(f == pl.num_programs(1) - 1)
    def _():
        o_ref[...] = _residual_layernorm(x_ref[...], o_ref[...], mod_ref[5:6, :], lng_ref[...], lnb_ref[...], alpha)


def _ffn_dense(h, wg, wu, wd, x, mod, ln_g, ln_b, alpha, seq):
    t, d = x.shape
    ff = wg.shape[1]
    tm = _tile(seq, FFN_ROWS)
    tf = _tile(ff, FFN_COLS)
    per_b = seq // tm
    return pl.pallas_call(
        functools.partial(_ffn_dense_kernel, alpha=alpha),
        grid=(t // tm, ff // tf),
        in_specs=[
            pl.BlockSpec((tm, d), lambda i, f: (i, 0)),
            pl.BlockSpec((d, tf), lambda i, f: (0, f)),
            pl.BlockSpec((d, tf), lambda i, f: (0, f)),
            pl.BlockSpec((tf, d), lambda i, f: (f, 0)),
            pl.BlockSpec((tm, d), lambda i, f: (i, 0), pipeline_mode=pl.Buffered(1)),
            pl.BlockSpec((None, N_MOD, d), lambda i, f: (i // per_b, 0, 0)),
            pl.BlockSpec((1, d), lambda i, f: (0, 0)),
            pl.BlockSpec((1, d), lambda i, f: (0, 0)),
        ],
        out_specs=pl.BlockSpec((tm, d), lambda i, f: (i, 0)),
        out_shape=jax.ShapeDtypeStruct((t, d), F32),
        compiler_params=_params(("parallel", "arbitrary")),
        name="ffn_dense_ln",
    )(h, wg, wu, wd, x, mod, ln_g.reshape(1, d), ln_b.reshape(1, d))


def _pack_bf16_pairs(v):
    n = v.shape[1] // 2
    lo = pltpu.bitcast(v[:, :n].astype(BF16).astype(F32), U32)
    hi = pltpu.bitcast(v[:, n:].astype(BF16).astype(F32), U32)
    return hi | (lo >> 16)


def _unpack_bf16_pairs(w):
    lo = pltpu.bitcast(w << 16, F32).astype(BF16)
    hi = pltpu.bitcast(w & jnp.uint32(0xFFFF0000), F32).astype(BF16)
    return jnp.concatenate([lo, hi], axis=1)


def _centred_window_sum(h, w):
    assert w >= 2 and w & (w - 1) == 0, w
    n = h.shape[0]
    f, m = h, 1
    while m < w // 2:
        f = f + pltpu.roll(f, n - m, axis=0)
        m *= 2
    return f + pltpu.roll(f, w // 2, axis=0)


def _pool_kernel(x_ref, xp_ref, xn_ref, mod_ref, pw_ref, ps_ref, lng_ref, lnb_ref, rt_ref,
                 xo_ref, hp_ref, route_ref, *, alpha, seq, n_exp):
    tm, d = x_ref.shape
    gdim = d // len(POOL_WINDOWS)
    sh, sc = mod_ref[0:1, :], mod_ref[1:2, :]
    s0 = (pl.program_id(0) * tm) % seq
    x = x_ref[...]
    hc = x * (1.0 + sc) + sh
    has_prev = (s0 > 0).astype(F32)
    has_next = (s0 + tm < seq).astype(F32)
    hcat = jnp.concatenate([(xp_ref[...] * (1.0 + sc) + sh) * has_prev, hc,
                            (xn_ref[...] * (1.0 + sc) + sh) * has_next], axis=0)
    pos = s0 + lax.broadcasted_iota(I32, (tm, 1), 0)
    ys = []
    for g, w in enumerate(POOL_WINDOWS):
        left = w // 2
        right = w - 1 - left
        cols = slice(g * gdim, (g + 1) * gdim)
        win = _centred_window_sum(hcat[:, cols], w)[POOL_HALO:POOL_HALO + tm]
        cnt = (jnp.minimum(pos + right, seq - 1) - jnp.maximum(pos - left, 0) + 1).astype(F32)
        dg = (win / cnt - hc[:, cols]).astype(BF16)
        ys.append(_dot(dg, pw_ref[g]))
    y = jnp.concatenate(ys, axis=1) * ps_ref[...]
    xn = _residual_layernorm(x, y, mod_ref[2:3, :], lng_ref[...], lnb_ref[...], alpha)
    xo_ref[...] = xn
    h2 = xn * (1.0 + mod_ref[4:5, :]) + mod_ref[3:4, :]
    hp_ref[...] = _pack_bf16_pairs(h2)
    h_hi = h2.astype(BF16)
    h_lo = (h2 - h_hi.astype(F32)).astype(BF16)
    hi_terms = _dot(h_hi, rt_ref[...])
    logits = hi_terms[:, :LANES] + hi_terms[:, LANES:] + _dot(h_lo, rt_ref[:, :LANES])
    lane = lax.broadcasted_iota(I32, logits.shape, 1)
    logits = jnp.where(lane < n_exp, logits, -jnp.inf)
    ex = jnp.exp(logits - jnp.max(logits, axis=-1, keepdims=True))
    probs = ex / jnp.sum(ex, axis=-1, keepdims=True)
    p1 = jnp.max(probs, axis=-1, keepdims=True)
    i1 = jnp.min(jnp.where(probs == p1, lane, LANES), axis=-1, keepdims=True)
    rest = jnp.where(lane == i1, -1.0, probs)
    p2 = jnp.max(rest, axis=-1, keepdims=True)
    i2 = jnp.min(jnp.where(rest == p2, lane, LANES), axis=-1, keepdims=True)
    den = p1 + p2
    route = jnp.where(lane == 0, i1.astype(F32), jnp.where(lane == 1, i2.astype(F32),
                      jnp.where(lane == 2, p1 / den, jnp.where(lane == 3, p2 / den, 0.0))))
    route_ref[...] = route


def _pool_layer(x, mod, pool_w, pool_scale, ln_g, ln_b, router, alpha, seq):
    t, d = x.shape
    n_exp = router.shape[1]
    tm = _tile(seq, 256)
    per_halo = tm // POOL_HALO
    n_halo = t // POOL_HALO
    rt_hi = router.astype(BF16)
    rt_lo = (router - rt_hi.astype(F32)).astype(BF16)
    rt = (jnp.zeros((d, 2 * LANES), BF16).at[:, :n_exp].set(rt_hi).at[:, LANES:LANES + n_exp].set(rt_lo))
    pw = pool_w.astype(BF16)
    return pl.pallas_call(
        functools.partial(_pool_kernel, alpha=alpha, seq=seq, n_exp=n_exp),
        grid=(t // tm,),
        in_specs=[
            pl.BlockSpec((tm, d), lambda i: (i, 0)),
            pl.BlockSpec((POOL_HALO, d), lambda i: (jnp.maximum(i * per_halo - 1, 0), 0)),
            pl.BlockSpec((POOL_HALO, d), lambda i: (jnp.minimum((i + 1) * per_halo, n_halo - 1), 0)),
            pl.BlockSpec((None, N_MOD, d), lambda i: (i * tm // seq, 0, 0)),
            pl.BlockSpec(pw.shape, lambda i: (0, 0, 0)),
            pl.BlockSpec((1, d), lambda i: (0, 0)),
            pl.BlockSpec((1, d), lambda i: (0, 0)),
            pl.BlockSpec((1, d), lambda i: (0, 0)),
            pl.BlockSpec((d, 2 * LANES), lambda i: (0, 0)),
        ],
        out_specs=[
            pl.BlockSpec((tm, d), lambda i: (i, 0)),
            pl.BlockSpec((tm, d // 2), lambda i: (i, 0)),
            pl.BlockSpec((tm, LANES), lambda i: (i, 0)),
        ],
        out_shape=[
            jax.ShapeDtypeStruct((t, d), F32),
            jax.ShapeDtypeStruct((t, d // 2), U32),
            jax.ShapeDtypeStruct((t, LANES), F32),
        ],
        compiler_params=_params(("parallel",)),
        name="pool_ln_router",
    )(x, x, x, mod, pw, pool_scale.reshape(1, d), ln_g.reshape(1, d), ln_b.reshape(1, d), rt)


def _route_tables(ids, n_exp, tm, n_tiles):
    t, k = ids.shape
    onehot = (ids[:, :, None] == jnp.arange(n_exp, dtype=I32)[None, None, :]).astype(I32).sum(axis=1)
    csum = jnp.cumsum(onehot, axis=0)
    rank = csum - onehot
    counts = csum[-1]
    padded = ((counts + tm - 1) // tm) * tm
    ends = jnp.cumsum(padded)
    offs = ends - padded
    slot = offs[ids] + jnp.take_along_axis(rank, ids, axis=1)
    tile_start = jnp.arange(n_tiles, dtype=I32) * tm
    tile_expert = jnp.minimum(jnp.sum(tile_start[:, None] >= ends[None, :], axis=1), n_exp - 1).astype(I32)
    tile_rows = jnp.clip((offs + counts)[tile_expert] - tile_start, 0, tm).astype(I32)
    n_valid = (ends[-1:] // tm).astype(I32)
    tok = jnp.zeros((n_tiles * tm,), I32).at[slot.reshape(-1)].set(jnp.repeat(jnp.arange(t, dtype=I32), k))
    return slot.astype(I32), tok, tile_expert, tile_rows, n_valid


def _ffn_group_kernel(te_ref, rows_ref, nv_ref, tok_ref, hp_ref, wg_ref, wu_ref, wd_ref, o_ref,
                      xbuf_ref, xb_ref, sem, *, rows_per_step):
    i, f = pl.program_id(0), pl.program_id(1)
    nf = pl.num_programs(1)
    tm = xb_ref.shape[0]
    half = tm // 2
    n_rows = nf * rows_per_step
    n_live = nv_ref[0]
    live = i < n_live
    full = rows_ref[i] > half
    first, last = f == 0, f == nf - 1

    def row_copy(src_row, r):
        return pltpu.make_async_copy(hp_ref.at[pl.ds(src_row, 1)], xbuf_ref.at[pl.ds(r, 1)], sem)

    @pl.when(first & (i == 0))
    def _():
        def start(r, carry):
            row_copy(tok_ref[r], r).start()
            return carry
        lax.fori_loop(0, n_rows, start, 0, unroll=8)

    @pl.when(first & (i <= n_live))
    def _():
        def wait(r, carry):
            row_copy(0, r).wait()
            return carry
        lax.fori_loop(0, n_rows, wait, 0, unroll=8)

    @pl.when(first & live)
    def _():
        xb_ref[...] = _unpack_bf16_pairs(xbuf_ref[:tm, :])
        o_ref[...] = jnp.zeros_like(o_ref)

    def step(rows):
        for u in range(rows_per_step):
            r = f * rows_per_step + u
            row_copy(tok_ref[(i + 1) * tm + r], r).start()
        _swiglu_step(xb_ref[:rows, :], wg_ref[...].astype(BF16), wu_ref[...].astype(BF16),
                     wd_ref[...].astype(BF16), o_ref.at[pl.ds(0, rows)])

    @pl.when(live & full)
    def _():
        step(tm)

    @pl.when(live & jnp.logical_not(full))
    def _():
        step(half)

    @pl.when(jnp.logical_not(live) & last)
    def _():
        o_ref[...] = jnp.zeros_like(o_ref)


def _ffn_grouped(hp, tok, tile_expert, tile_rows, n_valid, wg, wu, wd, tm, rows_per_step):
    half = hp.shape[1]
    d = 2 * half
    n_tiles = tile_expert.shape[0]
    ff = wg.shape[2]
    tf = _tile(ff, FFN_COLS)
    nf = ff // tf
    n_rows = nf * rows_per_step
    f_idx = lambda i, f, nv: jnp.where(i < nv[0], f, nf - 1)
    return pl.pallas_call(
        functools.partial(_ffn_group_kernel, rows_per_step=rows_per_step),
        grid_spec=pltpu.PrefetchScalarGridSpec(
            num_scalar_prefetch=4,
            grid=(n_tiles, nf),
            in_specs=[
                pl.BlockSpec(memory_space=pl.ANY),
                pl.BlockSpec((None, d, tf), lambda i, f, te, tr, nv, tok: (te[i], 0, f_idx(i, f, nv))),
                pl.BlockSpec((None, d, tf), lambda i, f, te, tr, nv, tok: (te[i], 0, f_idx(i, f, nv))),
                pl.BlockSpec((None, tf, d), lambda i, f, te, tr, nv, tok: (te[i], f_idx(i, f, nv), 0)),
            ],
            out_specs=pl.BlockSpec((tm, d), lambda i, f, te, tr, nv, tok: (i, 0)),
            scratch_shapes=[
                pltpu.VMEM((pl.cdiv(n_rows, 8) * 8, half), U32),
                pltpu.VMEM((tm, d), BF16),
                pltpu.SemaphoreType.DMA(()),
            ],
        ),
        out_shape=jax.ShapeDtypeStruct((n_tiles * tm, d), F32),
        compiler_params=_params(("arbitrary", "arbitrary")),
        name="moe_ffn_grouped",
    )(tile_expert, tile_rows, n_valid, tok, hp, wg, wu, wd)


def _combine_kernel(slot_ref, ys_ref, route_ref, x_ref, mod_ref, lng_ref, lnb_ref, o_ref, buf_ref, sem, *, alpha):
    tm = x_ref.shape[0]
    i = pl.program_id(0)
    cur = i % 2

    def row_copy(src_row, r, k, b):
        return pltpu.make_async_copy(ys_ref.at[pl.ds(src_row, 1)], buf_ref.at[b, k, pl.ds(r, 1)], sem.at[b])

    def wait_tile(b):
        def wait(r, carry):
            for k in range(TOP_K):
                row_copy(0, r, k, b).wait()
            return carry
        lax.fori_loop(0, tm, wait, 0, unroll=8)

    @pl.when(i == 0)
    def _():
        def start(r, carry):
            for k in range(TOP_K):
                row_copy(slot_ref[r * TOP_K + k], r, k, 0).start()
            return carry
        lax.fori_loop(0, tm, start, 0, unroll=8)

    wait_tile(cur)
    base = (i + 1) * tm * TOP_K
    for r in range(tm):
        for k in range(TOP_K):
            row_copy(slot_ref[base + r * TOP_K + k], r, k, 1 - cur).start()
    route = route_ref[...]
    y = buf_ref[cur, 0] * route[:, TOP_K:TOP_K + 1]
    for k in range(1, TOP_K):
        y = y + buf_ref[cur, k] * route[:, TOP_K + k:TOP_K + k + 1]
    o_ref[...] = _residual_layernorm(x_ref[...], y, mod_ref[5:6, :], lng_ref[...], lnb_ref[...], alpha)

    @pl.when(i == pl.num_programs(0) - 1)
    def _():
        wait_tile(1 - cur)


def _moe_combine(ys, slot, route, x, mod, ln_g, ln_b, alpha, seq):
    t, d = x.shape
    tm = _tile(seq, 256)
    per_b = seq // tm
    slot = jnp.concatenate([slot.reshape(-1), jnp.zeros((tm * TOP_K,), I32)])
    return pl.pallas_call(
        functools.partial(_combine_kernel, alpha=alpha),
        grid_spec=pltpu.PrefetchScalarGridSpec(
            num_scalar_prefetch=1,
            grid=(t // tm,),
            in_specs=[
                pl.BlockSpec(memory_space=pl.ANY),
                pl.BlockSpec((tm, LANES), lambda i, s: (i, 0)),
                pl.BlockSpec((tm, d), lambda i, s: (i, 0)),
                pl.BlockSpec((None, N_MOD, d), lambda i, s: (i // per_b, 0, 0)),
                pl.BlockSpec((1, d), lambda i, s: (0, 0)),
                pl.BlockSpec((1, d), lambda i, s: (0, 0)),
            ],
            out_specs=pl.BlockSpec((tm, d), lambda i, s: (i, 0)),
            scratch_shapes=[pltpu.VMEM((2, TOP_K, tm, d), F32), pltpu.SemaphoreType.DMA((2,))],
        ),
        out_shape=jax.ShapeDtypeStruct((t, d), F32),
        compiler_params=_params(("arbitrary",)),
        name="moe_combine_ln",
    )(slot, ys, route, x, mod, ln_g.reshape(1, d), ln_b.reshape(1, d))


def _moe_ffn(hp, route, x, mod, ln_g, ln_b, wg, wu, wd, alpha, seq):
    t = x.shape[0]
    n_exp, _, ff = wg.shape
    tm = _tile(t, FFN_ROWS)
    n_tiles = (t * TOP_K) // tm + n_exp
    rows_per_step = pl.cdiv(tm, ff // _tile(ff, FFN_COLS))
    ids = route[:, :TOP_K].astype(I32)
    slot, tok, tile_expert, tile_rows, n_valid = _route_tables(ids, n_exp, tm, n_tiles)
    tok = jnp.concatenate([tok, jnp.zeros((tm,), I32)])
    ys = _ffn_grouped(hp, tok, tile_expert, tile_rows, n_valid, wg, wu, wd, tm, rows_per_step)
    return _moe_combine(ys, slot, route, x, mod, ln_g, ln_b, alpha, seq)


def kernel(x, c, positions, mod_w, mod_b, ln_g, ln_b, mla_w_in, mla_q_norm, mla_w_uq, mla_kv_norm, mla_w_ukv, mla_w_o, pool_w, pool_scale, ffn_w_gate, ffn_w_up, ffn_w_down, moe_router, moe_w_gate, moe_w_up, moe_w_down):
    b, s, d = x.shape
    depth = mod_w.shape[0]
    alpha = (2.0 * depth) ** 0.25
    mod = _modulation(c, mod_w, mod_b)
    cos, sin = _rope_tables(positions)
    xt = x.reshape(b * s, d)
    for i in range(depth):
        j = i // 2
        if i % 2 == 0:
            q, k, v = _mla_project(xt, mod[i], cos, sin, mla_w_in[j], mla_q_norm[j], mla_w_uq[j],
                                   mla_kv_norm[j], mla_w_ukv[j], b, s)
            o = _attention(q, k, v).reshape(b * s, MLA_HEADS * V_HEAD)
            xt, h2 = _attn_out(o, mla_w_o[j], xt, mod[i], ln_g[i, 0], ln_b[i, 0], alpha, s)
            xt = _ffn_dense(h2, ffn_w_gate[j], ffn_w_up[j], ffn_w_down[j], xt, mod[i], ln_g[i, 1], ln_b[i, 1], alpha, s)
        else:
            xt, hp, route = _pool_layer(xt, mod[i], pool_w[j], pool_scale[j], ln_g[i, 0], ln_b[i, 0],
                                        moe_router[j], alpha, s)
            xt = _moe_ffn(hp, route, xt, mod[i], ln_g[i, 1], ln_b[i, 1],
                          moe_w_gate[j], moe_w_up[j], moe_w_down[j], alpha, s)
    return xt.reshape(b, s, d)
```

```python
import functools
import math

import jax
import jax.numpy as jnp
from jax import lax
from jax.experimental import pallas as pl
from jax.experimental.pallas import tpu as pltpu

F32 = jnp.float32
BF16 = jnp.bfloat16
U32 = jnp.uint32
I32 = jnp.int32

MLA_HEADS = 16
QK_NOPE = 128
QK_ROPE = 64
V_HEAD = 128
Q_LORA = 512
KV_LORA = 256
ROPE_THETA = 10000.0
POOL_WINDOWS = (2, 4, 8, 16)
TOP_K = 2
LN_EPS = 1e-5
RMS_EPS = 1e-6
N_MOD = 6

QK_DIM = QK_NOPE + QK_ROPE
POOL_HALO = 8
LANES = 128
VMEM_LIMIT = 56 * 1024 * 1024


def _tile(n, pref):
    t = min(n, pref)
    assert n % t == 0, (n, pref)
    return t


def _params(sem, vmem=VMEM_LIMIT):
    return pltpu.CompilerParams(dimension_semantics=sem, vmem_limit_bytes=vmem)


def _dot(a, b, **kw):
    return jnp.dot(a, b, preferred_element_type=F32, **kw)


def _silu(v):
    return v * jax.nn.sigmoid(v)


def _residual_layernorm(x, y, gate, ln_g, ln_b, alpha):
    z = alpha * x + (1.0 + gate) * y
    mu = jnp.mean(z, axis=-1, keepdims=True)
    zc = z - mu
    var = jnp.mean(zc * zc, axis=-1, keepdims=True)
    return zc * lax.rsqrt(var + LN_EPS) * ln_g + ln_b


def _rmsnorm(v, g):
    return v * lax.rsqrt(jnp.mean(v * v, axis=-1, keepdims=True) + RMS_EPS) * g


def _mod_kernel(c_ref, w_ref, b_ref, o_ref):
    ca = _silu(c_ref[...])
    o_ref[...] = _dot(ca, w_ref[...], precision=lax.Precision.HIGHEST) + b_ref[...]


def _modulation(c, mod_w, mod_b):
    depth, d, nd = mod_w.shape
    b = c.shape[0]
    rows = 8
    assert b <= rows
    c_pad = jnp.zeros((rows, d), F32).at[:b].set(c)
    tn = _tile(nd, 1024)
    out = pl.pallas_call(
        _mod_kernel,
        grid=(depth, nd // tn),
        in_specs=[
            pl.BlockSpec((rows, d), lambda i, n: (0, 0)),
            pl.BlockSpec((None, d, tn), lambda i, n: (i, 0, n)),
            pl.BlockSpec((None, 1, tn), lambda i, n: (i, 0, n)),
        ],
        out_specs=pl.BlockSpec((None, rows, tn), lambda i, n: (i, 0, n)),
        out_shape=jax.ShapeDtypeStruct((depth, rows, nd), F32),
        compiler_params=_params(("parallel", "parallel")),
        name="modulation",
    )(c_pad, mod_w, mod_b.reshape(depth, 1, nd))
    return out[:, :b].reshape(depth, b, N_MOD, d)


def _rope_kernel(pos_ref, freq_ref, cos_ref, sin_ref):
    ang = pos_ref[...] * freq_ref[...]
    cos_ref[...] = jnp.cos(ang)
    sin_ref[...] = jnp.sin(ang)


def _rope_tables(positions):
    t = positions.size
    half = QK_ROPE // 2
    inv_freq = ROPE_THETA ** (-jnp.arange(0, QK_ROPE, 2, dtype=F32) / QK_ROPE)
    freq = jnp.tile(inv_freq, LANES // half).reshape(1, LANES)
    pos = positions.astype(F32).reshape(t, 1)
    tm = _tile(t, 2048)
    return pl.pallas_call(
        _rope_kernel,
        grid=(t // tm,),
        in_specs=[pl.BlockSpec((tm, 1), lambda i: (i, 0)), pl.BlockSpec((1, LANES), lambda i: (0, 0))],
        out_specs=[pl.BlockSpec((tm, LANES), lambda i: (i, 0))] * 2,
        out_shape=[jax.ShapeDtypeStruct((t, LANES), F32)] * 2,
        compiler_params=_params(("parallel",)),
        name="rope_tables",
    )(pos, freq)


def _mla_proj_kernel(x_ref, mod_ref, cos_ref, sin_ref, win_ref, qn_ref, wqn_ref, wqr_ref, wqt_ref,
                     kvn_ref, wk_ref, wv_ref, q_ref, k_ref, v_ref, *, scale):
    sh, sc = mod_ref[0:1, :], mod_ref[1:2, :]
    h = (x_ref[...] * (1.0 + sc) + sh).astype(BF16)
    lat = _dot(h, win_ref[...])
    cos, sin = cos_ref[...], sin_ref[...]
    qn = _rmsnorm(lat[:, :Q_LORA], qn_ref[...]).astype(BF16)
    kn = _rmsnorm(lat[:, Q_LORA:Q_LORA + KV_LORA], kvn_ref[...]).astype(BF16)
    r0 = Q_LORA + KV_LORA
    k_rope = (lat[:, r0:r0 + QK_ROPE] * cos[:, :QK_ROPE]
              + lat[:, r0 + QK_ROPE:r0 + 2 * QK_ROPE] * sin[:, :QK_ROPE]).astype(BF16)
    q_nope = (_dot(qn, wqn_ref[...]) * scale).astype(BF16)
    q_r, q_t = _dot(qn, wqr_ref[...]), _dot(qn, wqt_ref[...])
    k_nope = _dot(kn, wk_ref[...]).astype(BF16)
    v_ref_val = _dot(kn, wv_ref[...]).astype(BF16)
    heads_per_vreg = LANES // QK_ROPE
    for hp in range(MLA_HEADS // heads_per_vreg):
        cols = slice(hp * LANES, (hp + 1) * LANES)
        roped = ((q_r[:, cols] * cos + q_t[:, cols] * sin) * scale).astype(BF16)
        for sub in range(heads_per_vreg):
            hd = hp * heads_per_vreg + sub
            q_ref[hd, :, :QK_NOPE] = q_nope[:, hd * QK_NOPE:(hd + 1) * QK_NOPE]
            q_ref[hd, :, QK_NOPE:] = roped[:, sub * QK_ROPE:(sub + 1) * QK_ROPE]
    for hd in range(MLA_HEADS):
        k_ref[hd, :, :QK_NOPE] = k_nope[:, hd * QK_NOPE:(hd + 1) * QK_NOPE]
        k_ref[hd, :, QK_NOPE:] = k_rope
        v_ref[hd, :, :V_HEAD] = v_ref_val[:, hd * V_HEAD:(hd + 1) * V_HEAD]
        v_ref[hd, :, V_HEAD:] = jnp.ones((v_ref.shape[1], V_HEAD), BF16)


def _rotate_half_cols(w):
    lead = w.shape[0]
    w = w.reshape(lead, -1, 2, QK_ROPE // 2)
    return jnp.concatenate([-w[:, :, 1:], w[:, :, :1]], axis=2).reshape(lead, -1)


def _mla_project(x, mod, cos, sin, w_in, q_norm, w_uq, kv_norm, w_ukv, batch, seq):
    t, d = x.shape
    hq = MLA_HEADS
    w_rope = w_in[:, Q_LORA + KV_LORA:]
    win = jnp.concatenate([w_in, _rotate_half_cols(w_rope)], axis=1).astype(BF16)
    wq = w_uq.reshape(Q_LORA, hq, QK_DIM)
    wq_nope = wq[:, :, :QK_NOPE].reshape(Q_LORA, hq * QK_NOPE).astype(BF16)
    wq_rope = wq[:, :, QK_NOPE:].reshape(Q_LORA, hq * QK_ROPE)
    wq_rot = _rotate_half_cols(wq_rope).astype(BF16)
    wq_rope = wq_rope.astype(BF16)
    wkv = w_ukv.reshape(KV_LORA, hq, QK_NOPE + V_HEAD)
    wk = wkv[:, :, :QK_NOPE].reshape(KV_LORA, hq * QK_NOPE).astype(BF16)
    wv = wkv[:, :, QK_NOPE:].reshape(KV_LORA, hq * V_HEAD).astype(BF16)
    tm = _tile(seq, 256)
    per_b = seq // tm
    full = lambda a: pl.BlockSpec(a.shape, lambda i: (0,) * a.ndim)
    qn2, kvn2 = q_norm.reshape(1, -1), kv_norm.reshape(1, -1)
    head_spec = lambda w: pl.BlockSpec((None, hq, tm, w), lambda i: (i // per_b, 0, i % per_b, 0))
    return pl.pallas_call(
        functools.partial(_mla_proj_kernel, scale=float(QK_DIM) ** -0.5 * math.log2(math.e)),
        grid=(t // tm,),
        in_specs=[
            pl.BlockSpec((tm, d), lambda i: (i, 0)),
            pl.BlockSpec((None, N_MOD, d), lambda i: (i // per_b, 0, 0)),
            pl.BlockSpec((tm, LANES), lambda i: (i, 0)),
            pl.BlockSpec((tm, LANES), lambda i: (i, 0)),
            full(win), full(qn2), full(wq_nope), full(wq_rope), full(wq_rot), full(kvn2), full(wk), full(wv),
        ],
        out_specs=[head_spec(QK_DIM), head_spec(QK_DIM), head_spec(2 * V_HEAD)],
        out_shape=[
            jax.ShapeDtypeStruct((batch, hq, seq, QK_DIM), BF16),
            jax.ShapeDtypeStruct((batch, hq, seq, QK_DIM), BF16),
            jax.ShapeDtypeStruct((batch, hq, seq, 2 * V_HEAD), BF16),
        ],
        compiler_params=_params(("parallel",)),
        name="mla_project",
    )(x, mod, cos, sin, win, qn2, wq_nope, wq_rope, wq_rot, kvn2, wk, wv)


ATTN_STAGES = 3


def _attn_kernel(q_ref, k_ref, v_ref, o_ref, s_ref, p_ref, m_ref, *, kc):
    rows, nk = s_ref.shape

    @pl.when(pl.program_id(0) == 0)
    def _():
        s_ref[...] = jnp.zeros_like(s_ref)
        p_ref[...] = jnp.ones_like(p_ref)
        m_ref[...] = jnp.zeros_like(m_ref)

    q = q_ref[...]
    m_prev = m_ref[...]
    m_run = None
    acc = None
    for j in range(nk // kc):
        keys = slice(j * kc, (j + 1) * kc)
        pv = _dot(p_ref[:, keys], v_ref[keys, :])
        acc = pv if acc is None else acc + pv
        for g in range(kc // LANES):
            cols = slice(j * kc + g * LANES, j * kc + (g + 1) * LANES)
            p_ref[:, cols] = jnp.exp2(s_ref[:, cols] - m_prev).astype(BF16)
        s = lax.dot_general(q, k_ref[keys, :], (((1,), (1,)), ((), ())), preferred_element_type=F32)
        s_ref[:, keys] = s
        for g in range(kc // LANES):
            sg = s[:, g * LANES:(g + 1) * LANES]
            m_run = sg if m_run is None else jnp.maximum(m_run, sg)
    o_ref[...] = (acc[:, :V_HEAD] * (1.0 / acc[:, V_HEAD:])).astype(o_ref.dtype)
    m_ref[...] = jnp.broadcast_to(jnp.max(m_run, axis=-1, keepdims=True), m_ref.shape)


def _attention(q, k, v):
    b, hq, s, _ = q.shape
    rows = _tile(s, 1024)
    per_head = s // rows
    n = b * hq * per_head
    lag = ATTN_STAGES - 1
    cur = lambda i: jnp.minimum(i, n - 1)
    old = lambda i: jnp.maximum(i - lag, 0)
    flat = lambda a: a.reshape(b * hq, s, a.shape[-1])
    return pl.pallas_call(
        functools.partial(_attn_kernel, kc=_tile(s, 512)),
        grid=(n + lag,),
        in_specs=[
            pl.BlockSpec((None, rows, QK_DIM), lambda i: (cur(i) // per_head, cur(i) % per_head, 0)),
            pl.BlockSpec((None, s, QK_DIM), lambda i: (cur(i) // per_head, 0, 0)),
            pl.BlockSpec((None, s, 2 * V_HEAD), lambda i: (old(i) // per_head, 0, 0)),
        ],
        out_specs=pl.BlockSpec((None, rows, V_HEAD),
                               lambda i: (old(i) // per_head // hq, old(i) % per_head, old(i) // per_head % hq)),
        out_shape=jax.ShapeDtypeStruct((b, s, hq * V_HEAD), BF16),
        scratch_shapes=[pltpu.VMEM((rows, s), F32), pltpu.VMEM((rows, s), BF16), pltpu.VMEM((rows, LANES), F32)],
        compiler_params=_params(("arbitrary",)),
        name="attention",
    )(flat(q), flat(k), flat(v))


def _attn_out_kernel(o_ref, wo_ref, x_ref, mod_ref, lng_ref, lnb_ref, xo_ref, ho_ref, *, alpha):
    y = _dot(o_ref[...], wo_ref[...])
    xn = _residual_layernorm(x_ref[...], y, mod_ref[2:3, :], lng_ref[...], lnb_ref[...], alpha)
    xo_ref[...] = xn
    ho_ref[...] = (xn * (1.0 + mod_ref[4:5, :]) + mod_ref[3:4, :]).astype(ho_ref.dtype)


def _attn_out(o, w_o, x, mod, ln_g, ln_b, alpha, seq):
    t, d = x.shape
    tm = _tile(seq, 512)
    per_b = seq // tm
    wo = w_o.astype(BF16)
    return pl.pallas_call(
        functools.partial(_attn_out_kernel, alpha=alpha),
        grid=(t // tm,),
        in_specs=[
            pl.BlockSpec((tm, o.shape[1]), lambda i: (i, 0)),
            pl.BlockSpec(wo.shape, lambda i: (0, 0)),
            pl.BlockSpec((tm, d), lambda i: (i, 0)),
            pl.BlockSpec((None, N_MOD, d), lambda i: (i // per_b, 0, 0)),
            pl.BlockSpec((1, d), lambda i: (0, 0)),
            pl.BlockSpec((1, d), lambda i: (0, 0)),
        ],
        out_specs=[pl.BlockSpec((tm, d), lambda i: (i, 0))] * 2,
        out_shape=[jax.ShapeDtypeStruct((t, d), F32), jax.ShapeDtypeStruct((t, d), BF16)],
        compiler_params=_params(("parallel",)),
        name="attn_out_ln",
    )(o, wo, x, mod, ln_g.reshape(1, d), ln_b.reshape(1, d))


FFN_ROWS = 1024
FFN_COLS = 256


def _swiglu_step(h, wg, wu, wd, acc_ref):
    g = _dot(h, wg)
    u = _dot(h, wu)
    a = (_silu(g) * u).astype(BF16)
    acc_ref[...] += _dot(a, wd)


def _ffn_dense_kernel(h_ref, wg_ref, wu_ref, wd_ref, x_ref, mod_ref, lng_ref, lnb_ref, o_ref, *, alpha):
    f = pl.program_id(1)

    @pl.when(f == 0)
    def _():
        o_ref[...] = jnp.zeros_like(o_ref)

    _swiglu_step(h_ref[...], wg_ref[...].astype(BF16), wu_ref[...].astype(BF16), wd_ref[...].astype(BF16), o_ref)

    @pl.when(f == pl.num_programs(1) - 1)
    def _():
        o_ref[...] = _residual_layernorm(x_ref[...], o_ref[...], mod_ref[5:6, :], lng_ref[...], lnb_ref[...], alpha)


def _ffn_dense(h, wg, wu, wd, layer, x, mod, ln_g, ln_b, alpha, seq):
    t, d = x.shape
    ff = wg.shape[2]
    tm = _tile(seq, FFN_ROWS)
    tf = _tile(ff, FFN_COLS)
    per_b = seq // tm
    return pl.pallas_call(
        functools.partial(_ffn_dense_kernel, alpha=alpha),
        grid=(t // tm, ff // tf),
        in_specs=[
            pl.BlockSpec((tm, d), lambda i, f: (i, 0)),
            pl.BlockSpec((None, d, tf), lambda i, f: (layer, 0, f)),
            pl.BlockSpec((None, d, tf), lambda i, f: (layer, 0, f)),
            pl.BlockSpec((None, tf, d), lambda i, f: (layer, f, 0)),
            pl.BlockSpec((tm, d), lambda i, f: (i, 0), pipeline_mode=pl.Buffered(1)),
            pl.BlockSpec((None, N_MOD, d), lambda i, f: (i // per_b, 0, 0)),
            pl.BlockSpec((1, d), lambda i, f: (0, 0)),
            pl.BlockSpec((1, d), lambda i, f: (0, 0)),
        ],
        out_specs=pl.BlockSpec((tm, d), lambda i, f: (i, 0)),
        out_shape=jax.ShapeDtypeStruct((t, d), F32),
        compiler_params=_params(("parallel", "arbitrary")),
        name="ffn_dense_ln",
    )(h, wg, wu, wd, x, mod, ln_g.reshape(1, d), ln_b.reshape(1, d))


def _pack_bf16_pairs(v):
    n = v.shape[1] // 2
    lo = pltpu.bitcast(v[:, :n].astype(BF16).astype(F32), U32)
    hi = pltpu.bitcast(v[:, n:].astype(BF16).astype(F32), U32)
    return hi | (lo >> 16)


def _unpack_bf16_pairs(w):
    lo = pltpu.bitcast(w << 16, F32).astype(BF16)
    hi = pltpu.bitcast(w & jnp.uint32(0xFFFF0000), F32).astype(BF16)
    return jnp.concatenate([lo, hi], axis=1)


def _centred_window_sum(h, w):
    assert w >= 2 and w & (w - 1) == 0, w
    n = h.shape[0]
    f, m = h, 1
    while m < w // 2:
        f = f + pltpu.roll(f, n - m, axis=0)
        m *= 2
    return f + pltpu.roll(f, w // 2, axis=0)


def _pool_kernel(x_ref, xp_ref, xn_ref, mod_ref, pw_ref, ps_ref, lng_ref, lnb_ref, rt_ref,
                 xo_ref, hp_ref, route_ref, *, alpha, seq, n_exp):
    tm, d = x_ref.shape
    gdim = d // len(POOL_WINDOWS)
    sh, sc = mod_ref[0:1, :], mod_ref[1:2, :]
    s0 = (pl.program_id(0) * tm) % seq
    x = x_ref[...]
    hc = x * (1.0 + sc) + sh
    has_prev = (s0 > 0).astype(F32)
    has_next = (s0 + tm < seq).astype(F32)
    hcat = jnp.concatenate([(xp_ref[...] * (1.0 + sc) + sh) * has_prev, hc,
                            (xn_ref[...] * (1.0 + sc) + sh) * has_next], axis=0)
    pos = s0 + lax.broadcasted_iota(I32, (tm, 1), 0)
    ys = []
    for g, w in enumerate(POOL_WINDOWS):
        left = w // 2
        right = w - 1 - left
        cols = slice(g * gdim, (g + 1) * gdim)
        win = _centred_window_sum(hcat[:, cols], w)[POOL_HALO:POOL_HALO + tm]
        cnt = (jnp.minimum(pos + right, seq - 1) - jnp.maximum(pos - left, 0) + 1).astype(F32)
        dg = (win / cnt - hc[:, cols]).astype(BF16)
        ys.append(_dot(dg, pw_ref[g]))
    y = jnp.concatenate(ys, axis=1) * ps_ref[...]
    xn = _residual_layernorm(x, y, mod_ref[2:3, :], lng_ref[...], lnb_ref[...], alpha)
    xo_ref[...] = xn
    h2 = xn * (1.0 + mod_ref[4:5, :]) + mod_ref[3:4, :]
    hp_ref[...] = _pack_bf16_pairs(h2)
    h_hi = h2.astype(BF16)
    h_lo = (h2 - h_hi.astype(F32)).astype(BF16)
    hi_terms = _dot(h_hi, rt_ref[...])
    logits = hi_terms[:, :LANES] + hi_terms[:, LANES:] + _dot(h_lo, rt_ref[:, :LANES])
    lane = lax.broadcasted_iota(I32, logits.shape, 1)
    logits = jnp.where(lane < n_exp, logits, -jnp.inf)
    ex = jnp.exp(logits - jnp.max(logits, axis=-1, keepdims=True))
    probs = ex / jnp.sum(ex, axis=-1, keepdims=True)
    p1 = jnp.max(probs, axis=-1, keepdims=True)
    i1 = jnp.min(jnp.where(probs == p1, lane, LANES), axis=-1, keepdims=True)
    rest = jnp.where(lane == i1, -1.0, probs)
    p2 = jnp.max(rest, axis=-1, keepdims=True)
    i2 = jnp.min(jnp.where(rest == p2, lane, LANES), axis=-1, keepdims=True)
    den = p1 + p2
    route = jnp.where(lane == 0, i1.astype(F32), jnp.where(lane == 1, i2.astype(F32),
                      jnp.where(lane == 2, p1 / den, jnp.where(lane == 3, p2 / den, 0.0))))
    route_ref[...] = route


def _pool_layer(x, mod, pool_w, pool_scale, ln_g, ln_b, router, alpha, seq):
    t, d = x.shape
    n_exp = router.shape[1]
    tm = _tile(seq, 256)
    per_halo = tm // POOL_HALO
    n_halo = t // POOL_HALO
    rt_hi = router.astype(BF16)
    rt_lo = (router - rt_hi.astype(F32)).astype(BF16)
    rt = (jnp.zeros((d, 2 * LANES), BF16).at[:, :n_exp].set(rt_hi).at[:, LANES:LANES + n_exp].set(rt_lo))
    pw = pool_w.astype(BF16)
    return pl.pallas_call(
        functools.partial(_pool_kernel, alpha=alpha, seq=seq, n_exp=n_exp),
        grid=(t // tm,),
        in_specs=[
            pl.BlockSpec((tm, d), lambda i: (i, 0)),
            pl.BlockSpec((POOL_HALO, d), lambda i: (jnp.maximum(i * per_halo - 1, 0), 0)),
            pl.BlockSpec((POOL_HALO, d), lambda i: (jnp.minimum((i + 1) * per_halo, n_halo - 1), 0)),
            pl.BlockSpec((None, N_MOD, d), lambda i: (i * tm // seq, 0, 0)),
            pl.BlockSpec(pw.shape, lambda i: (0, 0, 0)),
            pl.BlockSpec((1, d), lambda i: (0, 0)),
            pl.BlockSpec((1, d), lambda i: (0, 0)),
            pl.BlockSpec((1, d), lambda i: (0, 0)),
            pl.BlockSpec((d, 2 * LANES), lambda i: (0, 0)),
        ],
        out_specs=[
            pl.BlockSpec((tm, d), lambda i: (i, 0)),
            pl.BlockSpec((tm, d // 2), lambda i: (i, 0)),
            pl.BlockSpec((tm, LANES), lambda i: (i, 0)),
        ],
        out_shape=[
            jax.ShapeDtypeStruct((t, d), F32),
            jax.ShapeDtypeStruct((t, d // 2), U32),
            jax.ShapeDtypeStruct((t, LANES), F32),
        ],
        compiler_params=_params(("parallel",)),
        name="pool_ln_router",
    )(x, x, x, mod, pw, pool_scale.reshape(1, d), ln_g.reshape(1, d), ln_b.reshape(1, d), rt)


def _route_tables(ids, n_exp, tm, n_tiles):
    t, k = ids.shape
    onehot = (ids[:, :, None] == jnp.arange(n_exp, dtype=I32)[None, None, :]).astype(I32).sum(axis=1)
    csum = jnp.cumsum(onehot, axis=0)
    rank = csum - onehot
    counts = csum[-1]
    padded = ((counts + tm - 1) // tm) * tm
    ends = jnp.cumsum(padded)
    offs = ends - padded
    slot = offs[ids] + jnp.take_along_axis(rank, ids, axis=1)
    tile_start = jnp.arange(n_tiles, dtype=I32) * tm
    tile_expert = jnp.minimum(jnp.sum(tile_start[:, None] >= ends[None, :], axis=1), n_exp - 1).astype(I32)
    tile_rows = jnp.clip((offs + counts)[tile_expert] - tile_start, 0, tm).astype(I32)
    n_valid = (ends[-1:] // tm).astype(I32)
    tok = jnp.zeros((n_tiles * tm,), I32).at[slot.reshape(-1)].set(jnp.repeat(jnp.arange(t, dtype=I32), k))
    return slot.astype(I32), tok, tile_expert, tile_rows, n_valid


def _ffn_group_kernel(te_ref, rows_ref, nv_ref, tok_ref, hp_ref, wg_ref, wu_ref, wd_ref, o_ref,
                      xbuf_ref, xb_ref, sem, *, rows_per_step):
    i, f = pl.program_id(0), pl.program_id(1)
    nf = pl.num_programs(1)
    tm = xb_ref.shape[0]
    half = tm // 2
    n_rows = nf * rows_per_step
    n_live = nv_ref[0]
    live = i < n_live
    full = rows_ref[i] > half
    first, last = f == 0, f == nf - 1

    def row_copy(src_row, r):
        return pltpu.make_async_copy(hp_ref.at[pl.ds(src_row, 1)], xbuf_ref.at[pl.ds(r, 1)], sem)

    @pl.when(first & (i == 0))
    def _():
        def start(r, carry):
            row_copy(tok_ref[r], r).start()
            return carry
        lax.fori_loop(0, n_rows, start, 0, unroll=8)

    @pl.when(first & (i <= n_live))
    def _():
        def wait(r, carry):
            row_copy(0, r).wait()
            return carry
        lax.fori_loop(0, n_rows, wait, 0, unroll=8)

    @pl.when(first & live)
    def _():
        xb_ref[...] = _unpack_bf16_pairs(xbuf_ref[:tm, :])
        o_ref[...] = jnp.zeros_like(o_ref)

    def step(rows):
        for u in range(rows_per_step):
            r = f * rows_per_step + u
            row_copy(tok_ref[(i + 1) * tm + r], r).start()
        _swiglu_step(xb_ref[:rows, :], wg_ref[...].astype(BF16), wu_ref[...].astype(BF16),
                     wd_ref[...].astype(BF16), o_ref.at[pl.ds(0, rows)])

    @pl.when(live & full)
    def _():
        step(tm)

    @pl.when(live & jnp.logical_not(full))
    def _():
        step(half)

    @pl.when(jnp.logical_not(live) & last)
    def _():
        o_ref[...] = jnp.zeros_like(o_ref)


def _ffn_grouped(hp, tok, tile_expert, tile_rows, n_valid, wg, wu, wd, layer, tm, rows_per_step):
    half = hp.shape[1]
    d = 2 * half
    n_tiles = tile_expert.shape[0]
    ff = wg.shape[3]
    tf = _tile(ff, FFN_COLS)
    nf = ff // tf
    n_rows = nf * rows_per_step
    f_idx = lambda i, f, nv: jnp.where(i < nv[0], f, nf - 1)
    return pl.pallas_call(
        functools.partial(_ffn_group_kernel, rows_per_step=rows_per_step),
        grid_spec=pltpu.PrefetchScalarGridSpec(
            num_scalar_prefetch=4,
            grid=(n_tiles, nf),
            in_specs=[
                pl.BlockSpec(memory_space=pl.ANY),
                pl.BlockSpec((None, None, d, tf), lambda i, f, te, tr, nv, tok: (layer, te[i], 0, f_idx(i, f, nv))),
                pl.BlockSpec((None, None, d, tf), lambda i, f, te, tr, nv, tok: (layer, te[i], 0, f_idx(i, f, nv))),
                pl.BlockSpec((None, None, tf, d), lambda i, f, te, tr, nv, tok: (layer, te[i], f_idx(i, f, nv), 0)),
            ],
            out_specs=pl.BlockSpec((tm, d), lambda i, f, te, tr, nv, tok: (i, 0)),
            scratch_shapes=[
                pltpu.VMEM((pl.cdiv(n_rows, 8) * 8, half), U32),
                pltpu.VMEM((tm, d), BF16),
                pltpu.SemaphoreType.DMA(()),
            ],
        ),
        out_shape=jax.ShapeDtypeStruct((n_tiles * tm, d), F32),
        compiler_params=_params(("arbitrary", "arbitrary")),
        name="moe_ffn_grouped",
    )(tile_expert, tile_rows, n_valid, tok, hp, wg, wu, wd)


def _combine_kernel(slot_ref, ys_ref, route_ref, x_ref, mod_ref, lng_ref, lnb_ref, o_ref, buf_ref, sem, *, alpha):
    tm = x_ref.shape[0]
    i = pl.program_id(0)
    cur = i % 2

    def row_copy(src_row, r, k, b):
        return pltpu.make_async_copy(ys_ref.at[pl.ds(src_row, 1)], buf_ref.at[b, k, pl.ds(r, 1)], sem.at[b])

    def wait_tile(b):
        def wait(r, carry):
            for k in range(TOP_K):
                row_copy(0, r, k, b).wait()
            return carry
        lax.fori_loop(0, tm, wait, 0, unroll=8)

    @pl.when(i == 0)
    def _():
        def start(r, carry):
            for k in range(TOP_K):
                row_copy(slot_ref[r * TOP_K + k], r, k, 0).start()
            return carry
        lax.fori_loop(0, tm, start, 0, unroll=8)

    wait_tile(cur)
    base = (i + 1) * tm * TOP_K
    for r in range(tm):
        for k in range(TOP_K):
            row_copy(slot_ref[base + r * TOP_K + k], r, k, 1 - cur).start()
    route = route_ref[...]
    y = buf_ref[cur, 0] * route[:, TOP_K:TOP_K + 1]
    for k in range(1, TOP_K):
        y = y + buf_ref[cur, k] * route[:, TOP_K + k:TOP_K + k + 1]
    o_ref[...] = _residual_layernorm(x_ref[...], y, mod_ref[5:6, :], lng_ref[...], lnb_ref[...], alpha)

    @pl.when(i == pl.num_programs(0) - 1)
    def _():
        wait_tile(1 - cur)


def _moe_combine(ys, slot, route, x, mod, ln_g, ln_b, alpha, seq):
    t, d = x.shape
    tm = _tile(seq, 256)
    per_b = seq // tm
    slot = jnp.concatenate([slot.reshape(-1), jnp.zeros((tm * TOP_K,), I32)])
    return pl.pallas_call(
        functools.partial(_combine_kernel, alpha=alpha),
        grid_spec=pltpu.PrefetchScalarGridSpec(
            num_scalar_prefetch=1,
            grid=(t // tm,),
            in_specs=[
                pl.BlockSpec(memory_space=pl.ANY),
                pl.BlockSpec((tm, LANES), lambda i, s: (i, 0)),
                pl.BlockSpec((tm, d), lambda i, s: (i, 0)),
                pl.BlockSpec((None, N_MOD, d), lambda i, s: (i // per_b, 0, 0)),
                pl.BlockSpec((1, d), lambda i, s: (0, 0)),
                pl.BlockSpec((1, d), lambda i, s: (0, 0)),
            ],
            out_specs=pl.BlockSpec((tm, d), lambda i, s: (i, 0)),
            scratch_shapes=[pltpu.VMEM((2, TOP_K, tm, d), F32), pltpu.SemaphoreType.DMA((2,))],
        ),
        out_shape=jax.ShapeDtypeStruct((t, d), F32),
        compiler_params=_params(("arbitrary",)),
        name="moe_combine_ln",
    )(slot, ys, route, x, mod, ln_g.reshape(1, d), ln_b.reshape(1, d))


def _moe_ffn(hp, route, x, mod, ln_g, ln_b, wg, wu, wd, layer, alpha, seq):
    t = x.shape[0]
    _, n_exp, _, ff = wg.shape
    tm = _tile(t, FFN_ROWS)
    n_tiles = (t * TOP_K) // tm + n_exp
    rows_per_step = pl.cdiv(tm, ff // _tile(ff, FFN_COLS))
    ids = route[:, :TOP_K].astype(I32)
    slot, tok, tile_expert, tile_rows, n_valid = _route_tables(ids, n_exp, tm, n_tiles)
    tok = jnp.concatenate([tok, jnp.zeros((tm,), I32)])
    ys = _ffn_grouped(hp, tok, tile_expert, tile_rows, n_valid, wg, wu, wd, layer, tm, rows_per_step)
    return _moe_combine(ys, slot, route, x, mod, ln_g, ln_b, alpha, seq)


def kernel(x, c, positions, mod_w, mod_b, ln_g, ln_b, mla_w_in, mla_q_norm, mla_w_uq, mla_kv_norm, mla_w_ukv, mla_w_o, pool_w, pool_scale, ffn_w_gate, ffn_w_up, ffn_w_down, moe_router, moe_w_gate, moe_w_up, moe_w_down):
    b, s, d = x.shape
    depth = mod_w.shape[0]
    alpha = (2.0 * depth) ** 0.25
    mod = _modulation(c, mod_w, mod_b)
    cos, sin = _rope_tables(positions)
    xt = x.reshape(b * s, d)
    for i in range(depth):
        j = i // 2
        if i % 2 == 0:
            q, k, v = _mla_project(xt, mod[i], cos, sin, mla_w_in[j], mla_q_norm[j], mla_w_uq[j],
                                   mla_kv_norm[j], mla_w_ukv[j], b, s)
            o = _attention(q, k, v).reshape(b * s, MLA_HEADS * V_HEAD)
            xt, h2 = _attn_out(o, mla_w_o[j], xt, mod[i], ln_g[i, 0], ln_b[i, 0], alpha, s)
            xt = _ffn_dense(h2, ffn_w_gate, ffn_w_up, ffn_w_down, j, xt, mod[i], ln_g[i, 1], ln_b[i, 1], alpha, s)
        else:
            xt, hp, route = _pool_layer(xt, mod[i], pool_w[j], pool_scale[j], ln_g[i, 0], ln_b[i, 0],
                                        moe_router[j], alpha, s)
            xt = _moe_ffn(hp, route, xt, mod[i], ln_g[i, 1], ln_b[i, 1],
                          moe_w_gate, moe_w_up, moe_w_down, j, alpha, s)
    return xt.reshape(b, s, d)
```

```python
import functools
import math

import jax
import jax.numpy as jnp
from jax import lax
from jax.experimental import pallas as pl
from jax.experimental.pallas import tpu as pltpu

F32 = jnp.float32
BF16 = jnp.bfloat16
U32 = jnp.uint32
I32 = jnp.int32

MLA_HEADS = 16
QK_NOPE = 128
QK_ROPE = 64
V_HEAD = 128
Q_LORA = 512
KV_LORA = 256
ROPE_THETA = 10000.0
POOL_WINDOWS = (2, 4, 8, 16)
TOP_K = 2
LN_EPS = 1e-5
RMS_EPS = 1e-6
N_MOD = 6

QK_DIM = QK_NOPE + QK_ROPE
POOL_HALO = 8
LANES = 128
VMEM_LIMIT = 56 * 1024 * 1024


def _tile(n, pref):
    t = min(n, pref)
    assert n % t == 0, (n, pref)
    return t


def _params(sem, vmem=VMEM_LIMIT):
    return pltpu.CompilerParams(dimension_semantics=sem, vmem_limit_bytes=vmem)


def _dot(a, b, **kw):
    return jnp.dot(a, b, preferred_element_type=F32, **kw)


def _silu(v):
    return v * jax.nn.sigmoid(v)


def _residual_layernorm(x, y, gate, ln_g, ln_b, alpha):
    z = alpha * x + (1.0 + gate) * y
    mu = jnp.mean(z, axis=-1, keepdims=True)
    zc = z - mu
    var = jnp.mean(zc * zc, axis=-1, keepdims=True)
    return zc * lax.rsqrt(var + LN_EPS) * ln_g + ln_b


def _rmsnorm(v, g):
    return v * lax.rsqrt(jnp.mean(v * v, axis=-1, keepdims=True) + RMS_EPS) * g


def _mod_kernel(c_ref, w_ref, b_ref, o_ref):
    ca = _silu(c_ref[...])
    o_ref[...] = _dot(ca, w_ref[...], precision=lax.Precision.HIGHEST) + b_ref[...]


def _modulation(c, mod_w, mod_b):
    depth, d, nd = mod_w.shape
    b = c.shape[0]
    rows = 8
    assert b <= rows
    c_pad = jnp.zeros((rows, d), F32).at[:b].set(c)
    tn = _tile(nd, 1024)
    out = pl.pallas_call(
        _mod_kernel,
        grid=(depth, nd // tn),
        in_specs=[
            pl.BlockSpec((rows, d), lambda i, n: (0, 0)),
            pl.BlockSpec((None, d, tn), lambda i, n: (i, 0, n)),
            pl.BlockSpec((None, 1, tn), lambda i, n: (i, 0, n)),
        ],
        out_specs=pl.BlockSpec((None, rows, tn), lambda i, n: (i, 0, n)),
        out_shape=jax.ShapeDtypeStruct((depth, rows, nd), F32),
        compiler_params=_params(("parallel", "parallel")),
        name="modulation",
    )(c_pad, mod_w, mod_b.reshape(depth, 1, nd))
    return out[:, :b].reshape(depth, b, N_MOD, d)


def _rope_kernel(pos_ref, freq_ref, cos_ref, sin_ref):
    ang = pos_ref[...] * freq_ref[...]
    cos_ref[...] = jnp.cos(ang)
    sin_ref[...] = jnp.sin(ang)


def _rope_tables(positions):
    t = positions.size
    half = QK_ROPE // 2
    inv_freq = ROPE_THETA ** (-jnp.arange(0, QK_ROPE, 2, dtype=F32) / QK_ROPE)
    freq = jnp.tile(inv_freq, LANES // half).reshape(1, LANES)
    pos = positions.astype(F32).reshape(t, 1)
    tm = _tile(t, 2048)
    return pl.pallas_call(
        _rope_kernel,
        grid=(t // tm,),
        in_specs=[pl.BlockSpec((tm, 1), lambda i: (i, 0)), pl.BlockSpec((1, LANES), lambda i: (0, 0))],
        out_specs=[pl.BlockSpec((tm, LANES), lambda i: (i, 0))] * 2,
        out_shape=[jax.ShapeDtypeStruct((t, LANES), F32)] * 2,
        compiler_params=_params(("parallel",)),
        name="rope_tables",
    )(pos, freq)


def _mla_proj_kernel(x_ref, mod_ref, cos_ref, sin_ref, win_ref, qn_ref, wqn_ref, wqr_ref, wqt_ref,
                     kvn_ref, wk_ref, wv_ref, q_ref, k_ref, v_ref, *, scale):
    sh, sc = mod_ref[0:1, :], mod_ref[1:2, :]
    h = (x_ref[...] * (1.0 + sc) + sh).astype(BF16)
    lat = _dot(h, win_ref[...])
    cos, sin = cos_ref[...], sin_ref[...]
    qn = _rmsnorm(lat[:, :Q_LORA], qn_ref[...]).astype(BF16)
    kn = _rmsnorm(lat[:, Q_LORA:Q_LORA + KV_LORA], kvn_ref[...]).astype(BF16)
    r0 = Q_LORA + KV_LORA
    k_rope = (lat[:, r0:r0 + QK_ROPE] * cos[:, :QK_ROPE]
              + lat[:, r0 + QK_ROPE:r0 + 2 * QK_ROPE] * sin[:, :QK_ROPE]).astype(BF16)
    q_nope = (_dot(qn, wqn_ref[...]) * scale).astype(BF16)
    q_r, q_t = _dot(qn, wqr_ref[...]), _dot(qn, wqt_ref[...])
    k_nope = _dot(kn, wk_ref[...]).astype(BF16)
    v_ref_val = _dot(kn, wv_ref[...]).astype(BF16)
    heads_per_vreg = LANES // QK_ROPE
    for hp in range(MLA_HEADS // heads_per_vreg):
        cols = slice(hp * LANES, (hp + 1) * LANES)
        roped = ((q_r[:, cols] * cos + q_t[:, cols] * sin) * scale).astype(BF16)
        for sub in range(heads_per_vreg):
            hd = hp * heads_per_vreg + sub
            q_ref[hd, :, :QK_NOPE] = q_nope[:, hd * QK_NOPE:(hd + 1) * QK_NOPE]
            q_ref[hd, :, QK_NOPE:] = roped[:, sub * QK_ROPE:(sub + 1) * QK_ROPE]
    for hd in range(MLA_HEADS):
        k_ref[hd, :, :QK_NOPE] = k_nope[:, hd * QK_NOPE:(hd + 1) * QK_NOPE]
        k_ref[hd, :, QK_NOPE:] = k_rope
        v_ref[hd, :, :V_HEAD] = v_ref_val[:, hd * V_HEAD:(hd + 1) * V_HEAD]
        v_ref[hd, :, V_HEAD:] = jnp.ones((v_ref.shape[1], V_HEAD), BF16)


def _rotate_half_cols(w):
    lead = w.shape[0]
    w = w.reshape(lead, -1, 2, QK_ROPE // 2)
    return jnp.concatenate([-w[:, :, 1:], w[:, :, :1]], axis=2).reshape(lead, -1)


def _mla_project(x, mod, cos, sin, w_in, q_norm, w_uq, kv_norm, w_ukv, batch, seq):
    t, d = x.shape
    hq = MLA_HEADS
    w_rope = w_in[:, Q_LORA + KV_LORA:]
    win = jnp.concatenate([w_in, _rotate_half_cols(w_rope)], axis=1).astype(BF16)
    wq = w_uq.reshape(Q_LORA, hq, QK_DIM)
    wq_nope = wq[:, :, :QK_NOPE].reshape(Q_LORA, hq * QK_NOPE).astype(BF16)
    wq_rope = wq[:, :, QK_NOPE:].reshape(Q_LORA, hq * QK_ROPE)
    wq_rot = _rotate_half_cols(wq_rope).astype(BF16)
    wq_rope = wq_rope.astype(BF16)
    wkv = w_ukv.reshape(KV_LORA, hq, QK_NOPE + V_HEAD)
    wk = wkv[:, :, :QK_NOPE].reshape(KV_LORA, hq * QK_NOPE).astype(BF16)
    wv = wkv[:, :, QK_NOPE:].reshape(KV_LORA, hq * V_HEAD).astype(BF16)
    tm = _tile(seq, 256)
    per_b = seq // tm
    full = lambda a: pl.BlockSpec(a.shape, lambda i: (0,) * a.ndim)
    qn2, kvn2 = q_norm.reshape(1, -1), kv_norm.reshape(1, -1)
    head_spec = lambda w: pl.BlockSpec((None, hq, tm, w), lambda i: (i // per_b, 0, i % per_b, 0))
    return pl.pallas_call(
        functools.partial(_mla_proj_kernel, scale=float(QK_DIM) ** -0.5 * math.log2(math.e)),
        grid=(t // tm,),
        in_specs=[
            pl.BlockSpec((tm, d), lambda i: (i, 0)),
            pl.BlockSpec((None, N_MOD, d), lambda i: (i // per_b, 0, 0)),
            pl.BlockSpec((tm, LANES), lambda i: (i, 0)),
            pl.BlockSpec((tm, LANES), lambda i: (i, 0)),
            full(win), full(qn2), full(wq_nope), full(wq_rope), full(wq_rot), full(kvn2), full(wk), full(wv),
        ],
        out_specs=[head_spec(QK_DIM), head_spec(QK_DIM), head_spec(2 * V_HEAD)],
        out_shape=[
            jax.ShapeDtypeStruct((batch, hq, seq, QK_DIM), BF16),
            jax.ShapeDtypeStruct((batch, hq, seq, QK_DIM), BF16),
            jax.ShapeDtypeStruct((batch, hq, seq, 2 * V_HEAD), BF16),
        ],
        compiler_params=_params(("parallel",)),
        name="mla_project",
    )(x, mod, cos, sin, win, qn2, wq_nope, wq_rope, wq_rot, kvn2, wk, wv)


ATTN_STAGES = 3


def _attn_kernel(q_ref, k_ref, v_ref, o_ref, s_ref, p_ref, m_ref, *, kc):
    rows, nk = s_ref.shape

    @pl.when(pl.program_id(0) == 0)
    def _():
        s_ref[...] = jnp.zeros_like(s_ref)
        p_ref[...] = jnp.ones_like(p_ref)
        m_ref[...] = jnp.zeros_like(m_ref)

    q = q_ref[...]
    m_prev = m_ref[...]
    m_run = None
    acc = None
    for j in range(nk // kc):
        keys = slice(j * kc, (j + 1) * kc)
        pv = _dot(p_ref[:, keys], v_ref[keys, :])
        acc = pv if acc is None else acc + pv
        for g in range(kc // LANES):
            cols = slice(j * kc + g * LANES, j * kc + (g + 1) * LANES)
            p_ref[:, cols] = jnp.exp2(s_ref[:, cols] - m_prev).astype(BF16)
        s = lax.dot_general(q, k_ref[keys, :], (((1,), (1,)), ((), ())), preferred_element_type=F32)
        s_ref[:, keys] = s
        for g in range(kc // LANES):
            sg = s[:, g * LANES:(g + 1) * LANES]
            m_run = sg if m_run is None else jnp.maximum(m_run, sg)
    o_ref[...] = (acc[:, :V_HEAD] * (1.0 / acc[:, V_HEAD:])).astype(o_ref.dtype)
    m_ref[...] = jnp.broadcast_to(jnp.max(m_run, axis=-1, keepdims=True), m_ref.shape)


def _attention(q, k, v):
    b, hq, s, _ = q.shape
    rows = _tile(s, 1024)
    per_head = s // rows
    n = b * hq * per_head
    lag = ATTN_STAGES - 1
    cur = lambda i: jnp.minimum(i, n - 1)
    old = lambda i: jnp.maximum(i - lag, 0)
    flat = lambda a: a.reshape(b * hq, s, a.shape[-1])
    return pl.pallas_call(
        functools.partial(_attn_kernel, kc=_tile(s, 512)),
        grid=(n + lag,),
        in_specs=[
            pl.BlockSpec((None, rows, QK_DIM), lambda i: (cur(i) // per_head, cur(i) % per_head, 0)),
            pl.BlockSpec((None, s, QK_DIM), lambda i: (cur(i) // per_head, 0, 0)),
            pl.BlockSpec((None, s, 2 * V_HEAD), lambda i: (old(i) // per_head, 0, 0)),
        ],
        out_specs=pl.BlockSpec((None, rows, V_HEAD),
                               lambda i: (old(i) // per_head // hq, old(i) % per_head, old(i) // per_head % hq)),
        out_shape=jax.ShapeDtypeStruct((b, s, hq * V_HEAD), BF16),
        scratch_shapes=[pltpu.VMEM((rows, s), F32), pltpu.VMEM((rows, s), BF16), pltpu.VMEM((rows, LANES), F32)],
        compiler_params=_params(("arbitrary",)),
        name="attention",
    )(flat(q), flat(k), flat(v))


def _attn_out_kernel(o_ref, wo_ref, x_ref, mod_ref, lng_ref, lnb_ref, xo_ref, ho_ref, *, alpha):
    y = _dot(o_ref[...], wo_ref[...])
    xn = _residual_layernorm(x_ref[...], y, mod_ref[2:3, :], lng_ref[...], lnb_ref[...], alpha)
    xo_ref[...] = xn
    ho_ref[...] = (xn * (1.0 + mod_ref[4:5, :]) + mod_ref[3:4, :]).astype(ho_ref.dtype)


def _attn_out(o, w_o, x, mod, ln_g, ln_b, alpha, seq):
    t, d = x.shape
    tm = _tile(seq, 512)
    per_b = seq // tm
    wo = w_o.astype(BF16)
    return pl.pallas_call(
        functools.partial(_attn_out_kernel, alpha=alpha),
        grid=(t // tm,),
        in_specs=[
            pl.BlockSpec((tm, o.shape[1]), lambda i: (i, 0)),
            pl.BlockSpec(wo.shape, lambda i: (0, 0)),
            pl.BlockSpec((tm, d), lambda i: (i, 0)),
            pl.BlockSpec((None, N_MOD, d), lambda i: (i // per_b, 0, 0)),
            pl.BlockSpec((1, d), lambda i: (0, 0)),
            pl.BlockSpec((1, d), lambda i: (0, 0)),
        ],
        out_specs=[pl.BlockSpec((tm, d), lambda i: (i, 0))] * 2,
        out_shape=[jax.ShapeDtypeStruct((t, d), F32), jax.ShapeDtypeStruct((t, d), BF16)],
        compiler_params=_params(("parallel",)),
        name="attn_out_ln",
    )(o, wo, x, mod, ln_g.reshape(1, d), ln_b.reshape(1, d))


FFN_ROWS = 1024
FFN_COLS = 256
MOE_TILE_PARTS = 4


def _swiglu_step(h, wg, wu, wd, acc_ref):
    g = _dot(h, wg)
    u = _dot(h, wu)
    a = (_silu(g) * u).astype(BF16)
    acc_ref[...] += _dot(a, wd)


def _ffn_dense_kernel(h_ref, wg_ref, wu_ref, wd_ref, x_ref, mod_ref, lng_ref, lnb_ref, o_ref, *, alpha):
    f = pl.program_id(1)

    @pl.when(f == 0)
    def _():
        o_ref[...] = jnp.zeros_like(o_ref)

    _swiglu_step(h_ref[...], wg_ref[...].astype(BF16), wu_ref[...].astype(BF16), wd_ref[...].astype(BF16), o_ref)

    @pl.when(f == pl.num_programs(1) - 1)
    def _():
        o_ref[...] = _residual_layernorm(x_ref[...], o_ref[...], mod_ref[5:6, :], lng_ref[...], lnb_ref[...], alpha)


def _ffn_dense(h, wg, wu, wd, layer, x, mod, ln_g, ln_b, alpha, seq):
    t, d = x.shape
    ff = wg.shape[2]
    tm = _tile(seq, FFN_ROWS)
    tf = _tile(ff, FFN_COLS)
    per_b = seq // tm
    return pl.pallas_call(
        functools.partial(_ffn_dense_kernel, alpha=alpha),
        grid=(t // tm, ff // tf),
        in_specs=[
            pl.BlockSpec((tm, d), lambda i, f: (i, 0)),
            pl.BlockSpec((None, d, tf), lambda i, f: (layer, 0, f)),
            pl.BlockSpec((None, d, tf), lambda i, f: (layer, 0, f)),
            pl.BlockSpec((None, tf, d), lambda i, f: (layer, f, 0)),
            pl.BlockSpec((tm, d), lambda i, f: (i, 0), pipeline_mode=pl.Buffered(1)),
            pl.BlockSpec((None, N_MOD, d), lambda i, f: (i // per_b, 0, 0)),
            pl.BlockSpec((1, d), lambda i, f: (0, 0)),
            pl.BlockSpec((1, d), lambda i, f: (0, 0)),
        ],
        out_specs=pl.BlockSpec((tm, d), lambda i, f: (i, 0)),
        out_shape=jax.ShapeDtypeStruct((t, d), F32),
        compiler_params=_params(("parallel", "arbitrary")),
        name="ffn_dense_ln",
    )(h, wg, wu, wd, x, mod, ln_g.reshape(1, d), ln_b.reshape(1, d))


def _pack_bf16_pairs(v):
    n = v.shape[1] // 2
    lo = pltpu.bitcast(v[:, :n].astype(BF16).astype(F32), U32)
    hi = pltpu.bitcast(v[:, n:].astype(BF16).astype(F32), U32)
    return hi | (lo >> 16)


def _unpack_bf16_pairs(w):
    lo = pltpu.bitcast(w << 16, F32).astype(BF16)
    hi = pltpu.bitcast(w & jnp.uint32(0xFFFF0000), F32).astype(BF16)
    return jnp.concatenate([lo, hi], axis=1)


def _centred_window_sum(h, w):
    assert w >= 2 and w & (w - 1) == 0, w
    n = h.shape[0]
    f, m = h, 1
    while m < w // 2:
        f = f + pltpu.roll(f, n - m, axis=0)
        m *= 2
    return f + pltpu.roll(f, w // 2, axis=0)


def _pool_kernel(x_ref, xp_ref, xn_ref, mod_ref, pw_ref, ps_ref, lng_ref, lnb_ref, rt_ref,
                 xo_ref, hp_ref, route_ref, *, alpha, seq, n_exp):
    tm, d = x_ref.shape
    gdim = d // len(POOL_WINDOWS)
    sh, sc = mod_ref[0:1, :], mod_ref[1:2, :]
    s0 = (pl.program_id(0) * tm) % seq
    x = x_ref[...]
    hc = x * (1.0 + sc) + sh
    has_prev = (s0 > 0).astype(F32)
    has_next = (s0 + tm < seq).astype(F32)
    hcat = jnp.concatenate([(xp_ref[...] * (1.0 + sc) + sh) * has_prev, hc,
                            (xn_ref[...] * (1.0 + sc) + sh) * has_next], axis=0)
    pos = s0 + lax.broadcasted_iota(I32, (tm, 1), 0)
    ys = []
    for g, w in enumerate(POOL_WINDOWS):
        left = w // 2
        right = w - 1 - left
        cols = slice(g * gdim, (g + 1) * gdim)
        win = _centred_window_sum(hcat[:, cols], w)[POOL_HALO:POOL_HALO + tm]
        cnt = (jnp.minimum(pos + right, seq - 1) - jnp.maximum(pos - left, 0) + 1).astype(F32)
        dg = (win / cnt - hc[:, cols]).astype(BF16)
        ys.append(_dot(dg, pw_ref[g]))
    y = jnp.concatenate(ys, axis=1) * ps_ref[...]
    xn = _residual_layernorm(x, y, mod_ref[2:3, :], lng_ref[...], lnb_ref[...], alpha)
    xo_ref[...] = xn
    h2 = xn * (1.0 + mod_ref[4:5, :]) + mod_ref[3:4, :]
    hp_ref[...] = _pack_bf16_pairs(h2)
    h_hi = h2.astype(BF16)
    h_lo = (h2 - h_hi.astype(F32)).astype(BF16)
    hi_terms = _dot(h_hi, rt_ref[...])
    logits = hi_terms[:, :LANES] + hi_terms[:, LANES:] + _dot(h_lo, rt_ref[:, :LANES])
    lane = lax.broadcasted_iota(I32, logits.shape, 1)
    logits = jnp.where(lane < n_exp, logits, -jnp.inf)
    ex = jnp.exp(logits - jnp.max(logits, axis=-1, keepdims=True))
    probs = ex / jnp.sum(ex, axis=-1, keepdims=True)
    p1 = jnp.max(probs, axis=-1, keepdims=True)
    i1 = jnp.min(jnp.where(probs == p1, lane, LANES), axis=-1, keepdims=True)
    rest = jnp.where(lane == i1, -1.0, probs)
    p2 = jnp.max(rest, axis=-1, keepdims=True)
    i2 = jnp.min(jnp.where(rest == p2, lane, LANES), axis=-1, keepdims=True)
    den = p1 + p2
    route = jnp.where(lane == 0, i1.astype(F32), jnp.where(lane == 1, i2.astype(F32),
                      jnp.where(lane == 2, p1 / den, jnp.where(lane == 3, p2 / den, 0.0))))
    route_ref[...] = route


def _pool_layer(x, mod, pool_w, pool_scale, ln_g, ln_b, router, alpha, seq):
    t, d = x.shape
    n_exp = router.shape[1]
    tm = _tile(seq, 256)
    per_halo = tm // POOL_HALO
    n_halo = t // POOL_HALO
    rt_hi = router.astype(BF16)
    rt_lo = (router - rt_hi.astype(F32)).astype(BF16)
    rt = (jnp.zeros((d, 2 * LANES), BF16).at[:, :n_exp].set(rt_hi).at[:, LANES:LANES + n_exp].set(rt_lo))
    pw = pool_w.astype(BF16)
    return pl.pallas_call(
        functools.partial(_pool_kernel, alpha=alpha, seq=seq, n_exp=n_exp),
        grid=(t // tm,),
        in_specs=[
            pl.BlockSpec((tm, d), lambda i: (i, 0)),
            pl.BlockSpec((POOL_HALO, d), lambda i: (jnp.maximum(i * per_halo - 1, 0), 0)),
            pl.BlockSpec((POOL_HALO, d), lambda i: (jnp.minimum((i + 1) * per_halo, n_halo - 1), 0)),
            pl.BlockSpec((None, N_MOD, d), lambda i: (i * tm // seq, 0, 0)),
            pl.BlockSpec(pw.shape, lambda i: (0, 0, 0)),
            pl.BlockSpec((1, d), lambda i: (0, 0)),
            pl.BlockSpec((1, d), lambda i: (0, 0)),
            pl.BlockSpec((1, d), lambda i: (0, 0)),
            pl.BlockSpec((d, 2 * LANES), lambda i: (0, 0)),
        ],
        out_specs=[
            pl.BlockSpec((tm, d), lambda i: (i, 0)),
            pl.BlockSpec((tm, d // 2), lambda i: (i, 0)),
            pl.BlockSpec((tm, LANES), lambda i: (i, 0)),
        ],
        out_shape=[
            jax.ShapeDtypeStruct((t, d), F32),
            jax.ShapeDtypeStruct((t, d // 2), U32),
            jax.ShapeDtypeStruct((t, LANES), F32),
        ],
        compiler_params=_params(("parallel",)),
        name="pool_ln_router",
    )(x, x, x, mod, pw, pool_scale.reshape(1, d), ln_g.reshape(1, d), ln_b.reshape(1, d), rt)


def _route_tables(ids, n_exp, tm, n_tiles):
    t, k = ids.shape
    onehot = (ids[:, :, None] == jnp.arange(n_exp, dtype=I32)[None, None, :]).astype(I32).sum(axis=1)
    csum = jnp.cumsum(onehot, axis=0)
    rank = csum - onehot
    counts = csum[-1]
    padded = ((counts + tm - 1) // tm) * tm
    ends = jnp.cumsum(padded)
    offs = ends - padded
    slot = offs[ids] + jnp.take_along_axis(rank, ids, axis=1)
    tile_start = jnp.arange(n_tiles, dtype=I32) * tm
    tile_expert = jnp.minimum(jnp.sum(tile_start[:, None] >= ends[None, :], axis=1), n_exp - 1).astype(I32)
    tile_rows = jnp.clip((offs + counts)[tile_expert] - tile_start, 0, tm).astype(I32)
    n_valid = (ends[-1:] // tm).astype(I32)
    tok = jnp.zeros((n_tiles * tm,), I32).at[slot.reshape(-1)].set(jnp.repeat(jnp.arange(t, dtype=I32), k))
    return slot.astype(I32), tok, tile_expert, tile_rows, n_valid


def _ffn_group_kernel(te_ref, rows_ref, nv_ref, tok_ref, hp_ref, wg_ref, wu_ref, wd_ref, o_ref,
                      xbuf_ref, xb_ref, sem, *, rows_per_step):
    i, f = pl.program_id(0), pl.program_id(1)
    nf = pl.num_programs(1)
    tm = xb_ref.shape[0]
    part = tm // MOE_TILE_PARTS
    n_rows = nf * rows_per_step
    n_live = nv_ref[0]
    live = i < n_live
    n_parts = (rows_ref[i] + (part - 1)) // part
    first, last = f == 0, f == nf - 1

    def row_copy(src_row, r):
        return pltpu.make_async_copy(hp_ref.at[pl.ds(src_row, 1)], xbuf_ref.at[pl.ds(r, 1)], sem)

    @pl.when(first & (i == 0))
    def _():
        def start(r, carry):
            row_copy(tok_ref[r], r).start()
            return carry
        lax.fori_loop(0, n_rows, start, 0, unroll=8)

    @pl.when(first & (i <= n_live))
    def _():
        def wait(r, carry):
            row_copy(0, r).wait()
            return carry
        lax.fori_loop(0, n_rows, wait, 0, unroll=8)

    @pl.when(first & live)
    def _():
        xb_ref[...] = _unpack_bf16_pairs(xbuf_ref[:tm, :])
        o_ref[...] = jnp.zeros_like(o_ref)

    def step(rows):
        for u in range(rows_per_step):
            r = f * rows_per_step + u
            row_copy(tok_ref[(i + 1) * tm + r], r).start()
        _swiglu_step(xb_ref[:rows, :], wg_ref[...].astype(BF16), wu_ref[...].astype(BF16),
                     wd_ref[...].astype(BF16), o_ref.at[pl.ds(0, rows)])

    for parts in range(1, MOE_TILE_PARTS + 1):
        @pl.when(live & (n_parts == parts))
        def _(parts=parts):
            step(parts * part)

    @pl.when(jnp.logical_not(live) & last)
    def _():
        o_ref[...] = jnp.zeros_like(o_ref)


def _ffn_grouped(hp, tok, tile_expert, tile_rows, n_valid, wg, wu, wd, layer, tm, rows_per_step):
    half = hp.shape[1]
    d = 2 * half
    n_tiles = tile_expert.shape[0]
    ff = wg.shape[3]
    tf = _tile(ff, FFN_COLS)
    nf = ff // tf
    n_rows = nf * rows_per_step
    f_idx = lambda i, f, nv: jnp.where(i < nv[0], f, nf - 1)
    return pl.pallas_call(
        functools.partial(_ffn_group_kernel, rows_per_step=rows_per_step),
        grid_spec=pltpu.PrefetchScalarGridSpec(
            num_scalar_prefetch=4,
            grid=(n_tiles, nf),
            in_specs=[
                pl.BlockSpec(memory_space=pl.ANY),
                pl.BlockSpec((None, None, d, tf), lambda i, f, te, tr, nv, tok: (layer, te[i], 0, f_idx(i, f, nv))),
                pl.BlockSpec((None, None, d, tf), lambda i, f, te, tr, nv, tok: (layer, te[i], 0, f_idx(i, f, nv))),
                pl.BlockSpec((None, None, tf, d), lambda i, f, te, tr, nv, tok: (layer, te[i], f_idx(i, f, nv), 0)),
            ],
            out_specs=pl.BlockSpec((tm, d), lambda i, f, te, tr, nv, tok: (i, 0)),
            scratch_shapes=[
                pltpu.VMEM((pl.cdiv(n_rows, 8) * 8, half), U32),
                pltpu.VMEM((tm, d), BF16),
                pltpu.SemaphoreType.DMA(()),
            ],
        ),
        out_shape=jax.ShapeDtypeStruct((n_tiles * tm, d), F32),
        compiler_params=_params(("arbitrary", "arbitrary")),
        name="moe_ffn_grouped",
    )(tile_expert, tile_rows, n_valid, tok, hp, wg, wu, wd)


def _combine_kernel(slot_ref, ys_ref, route_ref, x_ref, mod_ref, lng_ref, lnb_ref, o_ref, buf_ref, sem, *, alpha):
    tm = x_ref.shape[0]
    i = pl.program_id(0)
    cur = i % 2

    def row_copy(src_row, r, k, b):
        return pltpu.make_async_copy(ys_ref.at[pl.ds(src_row, 1)], buf_ref.at[b, k, pl.ds(r, 1)], sem.at[b])

    def wait_tile(b):
        def wait(r, carry):
            for k in range(TOP_K):
                row_copy(0, r, k, b).wait()
            return carry
        lax.fori_loop(0, tm, wait, 0, unroll=8)

    @pl.when(i == 0)
    def _():
        def start(r, carry):
            for k in range(TOP_K):
                row_copy(slot_ref[r * TOP_K + k], r, k, 0).start()
            return carry
        lax.fori_loop(0, tm, start, 0, unroll=8)

    wait_tile(cur)
    base = (i + 1) * tm * TOP_K
    for r in range(tm):
        for k in range(TOP_K):
            row_copy(slot_ref[base + r * TOP_K + k], r, k, 1 - cur).start()
    route = route_ref[...]
    y = buf_ref[cur, 0] * route[:, TOP_K:TOP_K + 1]
    for k in range(1, TOP_K):
        y = y + buf_ref[cur, k] * route[:, TOP_K + k:TOP_K + k + 1]
    o_ref[...] = _residual_layernorm(x_ref[...], y, mod_ref[5:6, :], lng_ref[...], lnb_ref[...], alpha)

    @pl.when(i == pl.num_programs(0) - 1)
    def _():
        wait_tile(1 - cur)


def _moe_combine(ys, slot, route, x, mod, ln_g, ln_b, alpha, seq):
    t, d = x.shape
    tm = _tile(seq, 256)
    per_b = seq // tm
    slot = jnp.concatenate([slot.reshape(-1), jnp.zeros((tm * TOP_K,), I32)])
    return pl.pallas_call(
        functools.partial(_combine_kernel, alpha=alpha),
        grid_spec=pltpu.PrefetchScalarGridSpec(
            num_scalar_prefetch=1,
            grid=(t // tm,),
            in_specs=[
                pl.BlockSpec(memory_space=pl.ANY),
                pl.BlockSpec((tm, LANES), lambda i, s: (i, 0)),
                pl.BlockSpec((tm, d), lambda i, s: (i, 0)),
                pl.BlockSpec((None, N_MOD, d), lambda i, s: (i // per_b, 0, 0)),
                pl.BlockSpec((1, d), lambda i, s: (0, 0)),
                pl.BlockSpec((1, d), lambda i, s: (0, 0)),
            ],
            out_specs=pl.BlockSpec((tm, d), lambda i, s: (i, 0)),
            scratch_shapes=[pltpu.VMEM((2, TOP_K, tm, d), F32), pltpu.SemaphoreType.DMA((2,))],
        ),
        out_shape=jax.ShapeDtypeStruct((t, d), F32),
        compiler_params=_params(("arbitrary",)),
        name="moe_combine_ln",
    )(slot, ys, route, x, mod, ln_g.reshape(1, d), ln_b.reshape(1, d))


def _moe_ffn(hp, route, x, mod, ln_g, ln_b, wg, wu, wd, layer, alpha, seq):
    t = x.shape[0]
    _, n_exp, _, ff = wg.shape
    tm = _tile(t, FFN_ROWS)
    n_tiles = (t * TOP_K) // tm + n_exp
    rows_per_step = pl.cdiv(tm, ff // _tile(ff, FFN_COLS))
    ids = route[:, :TOP_K].astype(I32)
    slot, tok, tile_expert, tile_rows, n_valid = _route_tables(ids, n_exp, tm, n_tiles)
    tok = jnp.concatenate([tok, jnp.zeros((tm,), I32)])
    ys = _ffn_grouped(hp, tok, tile_expert, tile_rows, n_valid, wg, wu, wd, layer, tm, rows_per_step)
    return _moe_combine(ys, slot, route, x, mod, ln_g, ln_b, alpha, seq)


def kernel(x, c, positions, mod_w, mod_b, ln_g, ln_b, mla_w_in, mla_q_norm, mla_w_uq, mla_kv_norm, mla_w_ukv, mla_w_o, pool_w, pool_scale, ffn_w_gate, ffn_w_up, ffn_w_down, moe_router, moe_w_gate, moe_w_up, moe_w_down):
    b, s, d = x.shape
    depth = mod_w.shape[0]
    alpha = (2.0 * depth) ** 0.25
    mod = _modulation(c, mod_w, mod_b)
    cos, sin = _rope_tables(positions)
    xt = x.reshape(b * s, d)
    for i in range(depth):
        j = i // 2
        if i % 2 == 0:
            q, k, v = _mla_project(xt, mod[i], cos, sin, mla_w_in[j], mla_q_norm[j], mla_w_uq[j],
                                   mla_kv_norm[j], mla_w_ukv[j], b, s)
            o = _attention(q, k, v).reshape(b * s, MLA_HEADS * V_HEAD)
            xt, h2 = _attn_out(o, mla_w_o[j], xt, mod[i], ln_g[i, 0], ln_b[i, 0], alpha, s)
            xt = _ffn_dense(h2, ffn_w_gate, ffn_w_up, ffn_w_down, j, xt, mod[i], ln_g[i, 1], ln_b[i, 1], alpha, s)
        else:
            xt, hp, route = _pool_layer(xt, mod[i], pool_w[j], pool_scale[j], ln_g[i, 0], ln_b[i, 0],
                                        moe_router[j], alpha, s)
            xt = _moe_ffn(hp, route, xt, mod[i], ln_g[i, 1], ln_b[i, 1],
                          moe_w_gate, moe_w_up, moe_w_down, j, alpha, s)
    return xt.reshape(b, s, d)
```
